```python
import jax, jax.numpy as jnp
from jax import lax
import numpy as np

D_MODEL = 1024
BATCH = 8
SEQ = 2048
DEPTH = 4

CHUNK = 64
EPS = 1e-6
N_HEADS = 4
HEAD_DIM = 64
MIX_W = N_HEADS * HEAD_DIM
N_BRANCH = 4
ATT_LEFT_CHUNKS = 8
ATT_BAND = (ATT_LEFT_CHUNKS + 1) * CHUNK
REL_MAX = 256
REL_SIZE = REL_MAX + CHUNK
NEG_BIG = -1e30
HG_BLOCK = 16
LOG_FLOOR = 1e-30
GM_BLOCK = 128
GM_GROUPS = N_HEADS
GM_GROUP_W = MIX_W // GM_GROUPS
CONV_W = 4
LRU_C = 8.0
FFN_HIDDEN = ((8 * D_MODEL // 3 + 255) // 256) * 256
IN_SIZES = [MIX_W] * 11 + [N_BRANCH * D_MODEL]
IN_COLS = sum(IN_SIZES)

kernel_name = 'hybrid_chunk_causal_parallel_mixer'


def rms_norm(x, g):
    xf = x.astype(jnp.float32)
    y = xf * lax.rsqrt(jnp.mean(xf * xf, axis=-1, keepdims=True) + EPS)
    return (y * g.astype(jnp.float32)).astype(x.dtype)


def split_in(z):
    idx = np.cumsum(IN_SIZES)[:-1].tolist()
    return jnp.split(z, idx, axis=-1)


def chunk_band_attention(q, k, v, rel_bias):
    B, S, _ = q.shape
    nc = S // CHUNK
    L = ATT_LEFT_CHUNKS
    pad = L * CHUNK
    qc = q.reshape(B, nc, CHUNK, N_HEADS, HEAD_DIM)

    def band(t):
        tp = jnp.pad(t, ((0, 0), (pad, 0), (0, 0))).reshape(B, nc + L, CHUNK, N_HEADS, HEAD_DIM)
        return jnp.concatenate([tp[:, j:j + nc] for j in range(L + 1)], axis=2)

    kb, vb = band(k), band(v)
    s = jnp.einsum('bcqhd,bckhd->bchqk', qc, kb).astype(jnp.float32) * (HEAD_DIM ** -0.5)
    dist = pad + jnp.arange(CHUNK)[:, None] - jnp.arange(ATT_BAND)[None, :]
    idx = jnp.clip(dist, -(CHUNK - 1), REL_MAX) + (CHUNK - 1)
    bias = rel_bias.astype(jnp.float32)[:, idx]
    key_pos = jnp.arange(nc)[:, None] * CHUNK + jnp.arange(ATT_BAND)[None, :] - pad
    valid = key_pos >= 0
    s = jnp.where(valid[None, :, None, None, :], s + bias[None, None], NEG_BIG)
    p = jax.nn.softmax(s, axis=-1).astype(v.dtype)
    o = jnp.einsum('bchqk,bckhd->bcqhd', p, vb)
    return o.reshape(B, S, MIX_W)


def hgrn2(q, fz, i, g, lb, norm_g):
    B, S, _ = q.shape
    n = S // HG_BLOCK
    f32 = jnp.float32
    fz = fz.astype(f32)
    lb = lb.astype(f32)
    qf = jax.nn.silu(q.astype(f32))
    f = lb + (1.0 - lb) * jax.nn.sigmoid(fz)
    log_f = jnp.log(jnp.maximum(f, LOG_FLOOR))
    kf = (1.0 - lb) * jax.nn.sigmoid(-fz)
    shp = (B, n, HG_BLOCK, N_HEADS, HEAD_DIM)
    qf, kf, log_f = qf.reshape(shp), kf.reshape(shp), log_f.reshape(shp)
    vf = i.astype(f32).reshape(shp)
    b = jnp.cumsum(log_f, axis=2)
    causal = jnp.tril(jnp.ones((HG_BLOCK, HG_BLOCK), bool))[:, :, None, None]
    diff = b[:, :, :, None] - b[:, :, None, :]
    decay = jnp.where(causal, jnp.exp(jnp.where(causal, diff, 0.0)), 0.0)
    scores = jnp.einsum('bntshk,bnthk,bnshk->bnhts', decay, qf, kf)
    intra = jnp.einsum('bnhts,bnshv->bnthv', scores, vf)
    b_last = b[:, :, -1:]
    qd = qf * jnp.exp(b)
    kd = kf * jnp.exp(b_last - b)
    dec = jnp.exp(b_last[:, :, 0])

    def step(state, xs):
        qd_c, kd_c, v_c, dec_c = xs
        inter_c = jnp.einsum('bthk,bhkv->bthv', qd_c, state)
        state = dec_c[..., None] * state + jnp.einsum('bshk,bshv->bhkv', kd_c, v_c)
        return state, inter_c

    s0 = jnp.zeros((B, N_HEADS, HEAD_DIM, HEAD_DIM), f32)
    xs = (jnp.moveaxis(qd, 1, 0), jnp.moveaxis(kd, 1, 0), jnp.moveaxis(vf, 1, 0), jnp.moveaxis(dec, 1, 0))
    _, inter = lax.scan(step, s0, xs)
    o = (intra + jnp.moveaxis(inter, 0, 1)).reshape(B, S, N_HEADS, HEAD_DIM)
    o = o * lax.rsqrt(jnp.mean(o * o, axis=-1, keepdims=True) + EPS)
    o = o * norm_g.astype(f32).reshape(N_HEADS, HEAD_DIM)
    o = o.reshape(B, S, MIX_W) * jax.nn.silu(g.astype(f32))
    return o.astype(q.dtype)


def spatial_gating(u, v, norm_g, ws, bs):
    B, S, _ = u.shape
    n = S // GM_BLOCK
    vn = rms_norm(v, norm_g).reshape(B, n, GM_BLOCK, GM_GROUPS, GM_GROUP_W)
    w = ws * jnp.tril(jnp.ones((GM_BLOCK, GM_BLOCK), ws.dtype))
    mixed = jnp.einsum('gpq,bnqgc->bnpgc', w, vn) + bs.T[:, :, None]
    return u * mixed.reshape(B, S, MIX_W)


def rg_lru_branch(xin, gate, conv_w, conv_b, wa, ba, wx, bx, lam):
    B, S, _ = xin.shape
    f32 = jnp.float32
    xp = jnp.pad(xin, ((0, 0), (CONV_W - 1, 0), (0, 0)))
    xc = conv_b + xp[:, 0:S] * conv_w[0]
    for j in range(1, CONV_W):
        xc = xc + xp[:, j:j + S] * conv_w[j]
    xh = xc.reshape(B, S, N_HEADS, HEAD_DIM)
    r = jax.nn.sigmoid(jnp.einsum('bshi,hij->bshj', xh, wa).reshape(B, S, MIX_W) + ba)
    ig = jax.nn.sigmoid(jnp.einsum('bshi,hij->bshj', xh, wx).reshape(B, S, MIX_W) + bx)
    log_a = -LRU_C * r.astype(f32) * jax.nn.softplus(-lam.astype(f32))
    a = jnp.exp(log_a)
    mult = jnp.sqrt(jnp.maximum(-jnp.expm1(2.0 * log_a), 0.0))
    mult = jnp.where(jnp.arange(S)[None, :, None] == 0, 1.0, mult)
    bt = mult * (ig * xc).astype(f32)

    def combine(left, right):
        a1, b1 = left
        a2, b2 = right
        return a1 * a2, a2 * b1 + b2

    _, h = lax.associative_scan(combine, (a, bt), axis=1)
    return (h * jax.nn.gelu(gate.astype(f32))).astype(xin.dtype)


def hybrid_mixer(h, w_in, rel_bias, lb, hg_norm_g, gm_norm_g, gm_ws, gm_bs,
                 conv_w, conv_b, wa, ba, wx, bx, lam, w_branch, w_out):
    B, S, _ = h.shape
    z = h @ w_in
    aq, ak, av, bq, bf, bi, bg, cu, cv, dx, dg, gates = split_in(z)
    o_a = chunk_band_attention(aq, ak, av, rel_bias)
    o_b = hgrn2(bq, bf, bi, bg, lb, hg_norm_g)
    o_c = spatial_gating(jax.nn.gelu(cu), jax.nn.gelu(cv), gm_norm_g, gm_ws, gm_bs)
    o_d = rg_lru_branch(dx, dg, conv_w, conv_b, wa, ba, wx, bx, lam)
    outs = jnp.stack([o_a, o_b, o_c, o_d], axis=2)
    proj = jnp.einsum('bsnw,nwd->bsnd', outs, w_branch)
    g = jax.nn.sigmoid(gates.reshape(B, S, N_BRANCH, D_MODEL))
    merged = jnp.sum(g * proj, axis=2)
    return merged @ w_out


def swiglu(h, w1, w2):
    gt, up = jnp.split(h @ w1, 2, axis=-1)
    return (jax.nn.silu(gt) * up) @ w2


def setup_inputs(seed: int = 0) -> dict:
    key = jax.random.key(seed)
    ks = jax.random.split(key, 24)
    f32 = jnp.float32

    def nrm(k, shape, scale):
        return jax.random.normal(k, shape, f32) * scale

    u = jax.random.uniform(ks[18], (DEPTH, MIX_W), f32, 0.9, 0.999)
    sa = u ** (1.0 / LRU_C)
    return {
        'x': nrm(ks[0], (BATCH, SEQ, D_MODEL), 1.0),
        'norm_mix_pre': 1.0 + nrm(ks[1], (DEPTH, D_MODEL), 0.02),
        'norm_mix_post': 1.0 + nrm(ks[2], (DEPTH, D_MODEL), 0.02),
        'norm_ffn_pre': 1.0 + nrm(ks[3], (DEPTH, D_MODEL), 0.02),
        'norm_ffn_post': 1.0 + nrm(ks[4], (DEPTH, D_MODEL), 0.02),
        'w_in': nrm(ks[5], (DEPTH, D_MODEL, IN_COLS), D_MODEL ** -0.5),
        'attn_rel_bias': nrm(ks[6], (DEPTH, N_HEADS, REL_SIZE), 0.1),
        'hgrn_lb_logits': nrm(ks[7], (DEPTH, MIX_W), 1.0),
        'hgrn_norm_g': 1.0 + nrm(ks[8], (DEPTH, MIX_W), 0.02),
        'gmlp_norm_g': 1.0 + nrm(ks[9], (DEPTH, MIX_W), 0.02),
        'gmlp_ws': nrm(ks[10], (DEPTH, GM_GROUPS, GM_BLOCK, GM_BLOCK), 0.5 * GM_BLOCK ** -0.5),
        'gmlp_bs': 1.0 + nrm(ks[11], (DEPTH, GM_GROUPS, GM_BLOCK), 0.01),
        'lru_conv_w': nrm(ks[12], (DEPTH, CONV_W, MIX_W), CONV_W ** -0.5),
        'lru_conv_b': nrm(ks[13], (DEPTH, MIX_W), 0.01),
        'lru_wa': nrm(ks[14], (DEPTH, N_HEADS, HEAD_DIM, HEAD_DIM), HEAD_DIM ** -0.5),
        'lru_ba': nrm(ks[15], (DEPTH, MIX_W), 0.01),
        'lru_wx': nrm(ks[16], (DEPTH, N_HEADS, HEAD_DIM, HEAD_DIM), HEAD_DIM ** -0.5),
        'lru_bx': nrm(ks[17], (DEPTH, MIX_W), 0.01),
        'lru_lambda': jnp.log(sa) - jnp.log1p(-sa),
        'w_branch': nrm(ks[19], (DEPTH, N_BRANCH, MIX_W, D_MODEL), MIX_W ** -0.5),
        'w_out': nrm(ks[20], (DEPTH, D_MODEL, D_MODEL), D_MODEL ** -0.5),
        'w_ffn_in': nrm(ks[21], (DEPTH, D_MODEL, 2 * FFN_HIDDEN), D_MODEL ** -0.5),
        'w_ffn_out': nrm(ks[22], (DEPTH, FFN_HIDDEN, D_MODEL), FFN_HIDDEN ** -0.5),
    }


def reference(x, norm_mix_pre, norm_mix_post, norm_ffn_pre, norm_ffn_post, w_in,
              attn_rel_bias, hgrn_lb_logits, hgrn_norm_g, gmlp_norm_g, gmlp_ws, gmlp_bs,
              lru_conv_w, lru_conv_b, lru_wa, lru_ba, lru_wx, lru_bx, lru_lambda,
              w_branch, w_out, w_ffn_in, w_ffn_out):
    p = jax.nn.softmax(hgrn_lb_logits.astype(jnp.float32), axis=0)
    lbs = jnp.cumsum(p, axis=0) - p[0]
    for l in range(DEPTH):
        h = rms_norm(x, norm_mix_pre[l])
        y = hybrid_mixer(h, w_in[l], attn_rel_bias[l], lbs[l], hgrn_norm_g[l], gmlp_norm_g[l],
                         gmlp_ws[l], gmlp_bs[l], lru_conv_w[l], lru_conv_b[l], lru_wa[l],
                         lru_ba[l], lru_wx[l], lru_bx[l], lru_lambda[l], w_branch[l], w_out[l])
        x = x + rms_norm(y, norm_mix_post[l])
        h = rms_norm(x, norm_ffn_pre[l])
        x = x + rms_norm(swiglu(h, w_ffn_in[l], w_ffn_out[l]), norm_ffn_post[l])
    return x
```

```python
import functools

import numpy as np
import jax
import jax.numpy as jnp
from jax import lax
from jax.experimental import pallas as pl
from jax.experimental.pallas import tpu as pltpu

D_MODEL = 1024
DEPTH = 4
CHUNK = 64
EPS = 1e-6
N_HEADS = 4
HEAD_DIM = 64
MIX_W = N_HEADS * HEAD_DIM
N_BRANCH = 4
ATT_LEFT_CHUNKS = 8
REL_MAX = 256
NEG_BIG = -1e30
LOG_FLOOR = 1e-30
GM_BLOCK = 128
CONV_W = 4
LRU_C = 8.0
FFN_HIDDEN = 2816
N_MIX_COLS = 11 * MIX_W

F32 = jnp.float32
BF16 = jnp.bfloat16

VMEM_LIMIT_BYTES = 56 * 1024 * 1024
TOKEN_TILE = 512
ATT_QB = 256
ATT_PAD = ATT_LEFT_CHUNKS * CHUNK
ATT_KB = ATT_PAD + ATT_QB
HG_CHUNK = 256
HG_LEVELS = (1, 2, 4, 8, 16, 32, 64, 128)


def _compiler_params(n_grid):
    return pltpu.CompilerParams(
        dimension_semantics=("arbitrary",) * n_grid,
        vmem_limit_bytes=VMEM_LIMIT_BYTES)


def _rms(x, g):
    return x * lax.rsqrt(jnp.mean(x * x, axis=-1, keepdims=True) + EPS) * g


def _gelu(x):
    return 0.5 * x * (1.0 + jnp.tanh(0.7978845608028654 * (x + 0.044715 * x * x * x)))


def _sigmoid(x):
    return 1.0 / (1.0 + jnp.exp(-x))


def _silu(x):
    return x * _sigmoid(x)


def _dot(a, b):
    return jnp.dot(a, b, preferred_element_type=F32)


def _dot_nt(a, b):
    return lax.dot_general(a, b, (((1,), (1,)), ((), ())), preferred_element_type=F32)


def _dot_tn(a, b):
    return lax.dot_general(a, b, (((0,), (0,)), ((), ())), preferred_element_type=F32)


def _split3(x):
    hi = x.astype(BF16)
    r = x - hi.astype(F32)
    mid = r.astype(BF16)
    lo = (r - mid.astype(F32)).astype(BF16)
    return hi, mid, lo


def _dot_exact_lhs(m_bf16, x):
    hi, mid, lo = _split3(x)
    return _dot(m_bf16, hi) + _dot(m_bf16, mid) + _dot(m_bf16, lo)


def _dot_exact_rhs(x, m_bf16):
    hi, mid, lo = _split3(x)
    return _dot(hi, m_bf16) + _dot(mid, m_bf16) + _dot(lo, m_bf16)


def _head_of_lane(shape, dim):
    return lax.broadcasted_iota(jnp.int32, shape, dim) // HEAD_DIM


def _in_proj_kernel(x_ref, g_ref, w_ref, z_ref):
    h = _rms(x_ref[...], g_ref[...]).astype(BF16)
    z_ref[...] = _dot(h, w_ref[...])


def _in_proj(x2, g, w_mix):
    t = x2.shape[0]
    return pl.pallas_call(
        _in_proj_kernel,
        grid=(t // TOKEN_TILE,),
        in_specs=[
            pl.BlockSpec((TOKEN_TILE, D_MODEL), lambda i: (i, 0)),
            pl.BlockSpec((1, D_MODEL), lambda i: (0, 0)),
            pl.BlockSpec((D_MODEL, N_MIX_COLS), lambda i: (0, 0)),
        ],
        out_specs=pl.BlockSpec((TOKEN_TILE, N_MIX_COLS), lambda i: (i, 0)),
        out_shape=jax.ShapeDtypeStruct((t, N_MIX_COLS), F32),
        compiler_params=_compiler_params(1),
        name="in_proj",
    )(x2, g, w_mix)


def _attn_kernel(q_ref, k_ref, v_ref, bias_ref, o_ref, kpad, vpad):
    s_len = q_ref.shape[0]
    kpad[0:ATT_PAD, :] = jnp.zeros((ATT_PAD, MIX_W), BF16)
    vpad[0:ATT_PAD, :] = jnp.zeros((ATT_PAD, MIX_W), BF16)
    kpad[ATT_PAD:, :] = k_ref[...].astype(BF16)
    vpad[ATT_PAD:, :] = v_ref[...].astype(BF16)
    lane_head = _head_of_lane((ATT_QB, MIX_W), 1)
    col = lax.broadcasted_iota(jnp.int32, (ATT_QB, ATT_KB), 1)

    def body(j, carry):
        r0 = pl.multiple_of(j * ATT_QB, ATT_QB)
        q = q_ref[pl.ds(r0, ATT_QB), :] * (HEAD_DIM ** -0.5)
        kb = kpad[pl.ds(r0, ATT_KB), :]
        vb = vpad[pl.ds(r0, ATT_KB), :]
        in_seq = col >= ATT_PAD - r0
        acc = jnp.zeros((ATT_QB, MIX_W), F32)
        for h in range(N_HEADS):
            qm = jnp.where(lane_head == h, q, 0.0).astype(BF16)
            s = _dot_nt(qm, kb) + bias_ref[h]
            s = jnp.where(in_seq, s, NEG_BIG)
            m = jnp.max(s, axis=-1, keepdims=True)
            p = jnp.exp(s - m)
            l = jnp.sum(p, axis=-1, keepdims=True)
            o = _dot(p.astype(BF16), vb)
            acc = acc + jnp.where(lane_head == h, o * (1.0 / l), 0.0)
        o_ref[pl.ds(r0, ATT_QB), :] = acc.astype(o_ref.dtype)
        return carry

    lax.fori_loop(0, s_len // ATT_QB, body, 0)


def _attn_bias_tile(rel_bias):
    qi = np.arange(ATT_QB)[:, None]
    kc = np.arange(ATT_KB)[None, :]
    dist = qi + ATT_PAD - kc
    idx = np.clip(dist, -(CHUNK - 1), REL_MAX) + (CHUNK - 1)
    dchunk = ATT_LEFT_CHUNKS + qi // CHUNK - kc // CHUNK
    band = (dchunk >= 0) & (dchunk <= ATT_LEFT_CHUNKS)
    bias = rel_bias.astype(F32)[:, idx]
    return jnp.where(band[None], bias, NEG_BIG)


def _attention(z3, rel_bias):
    b, s, _ = z3.shape
    col_spec = lambda c: pl.BlockSpec((None, s, MIX_W), lambda i, c=c: (i, 0, c))
    return pl.pallas_call(
        _attn_kernel,
        grid=(b,),
        in_specs=[col_spec(0), col_spec(1), col_spec(2),
                  pl.BlockSpec((N_HEADS, ATT_QB, ATT_KB), lambda i: (0, 0, 0))],
        out_specs=pl.BlockSpec((None, s, MIX_W), lambda i: (i, 0, 0)),
        out_shape=jax.ShapeDtypeStruct((b, s, MIX_W), BF16),
        scratch_shapes=[pltpu.VMEM((ATT_PAD + s, MIX_W), BF16),
                        pltpu.VMEM((ATT_PAD + s, MIX_W), BF16)],
        compiler_params=_compiler_params(1),
        name="attention",
    )(z3, z3, z3, _attn_bias_tile(rel_bias))


def _hgrn_coef_matrix():
    c = HG_CHUNK
    t = np.arange(c)[:, None]
    r = np.arange(c)[None, :]
    mats = [(r <= t)]
    for g in HG_LEVELS:
        bnd = (t // (2 * g)) * (2 * g) + g - 1
        right = (t // g) % 2 == 1
        m_right = (r > bnd) & (r <= t)
        m_left = (r > t) & (r <= bnd)
        mats.append(np.where(right, m_right, m_left))
    return np.concatenate(mats, axis=0).astype(np.float32)


def _hgrn_kernel(layer, q_ref, f_ref, i_ref, g_ref, lbl_ref, ng_ref, coef_ref, o_ref, st_ref):
    s_len = q_ref.shape[0]
    c = HG_CHUNK
    logits = lbl_ref[...]
    e = jnp.exp(logits - jnp.max(logits, axis=0, keepdims=True))
    p = e / jnp.sum(e, axis=0, keepdims=True)
    lb = jnp.zeros((1, MIX_W), F32)
    for l in range(1, layer + 1):
        lb = lb + p[l:l + 1, :]

    st_ref[...] = jnp.zeros((MIX_W, MIX_W), F32)
    row = lax.broadcasted_iota(jnp.int32, (c, c), 0)
    colm = lax.broadcasted_iota(jnp.int32, (c, c), 1)
    lane_head = _head_of_lane((c, MIX_W), 1)
    same_head = (_head_of_lane((MIX_W, MIX_W), 0) == _head_of_lane((MIX_W, MIX_W), 1))
    head_ones = jnp.where(same_head, 1.0, 0.0).astype(BF16)
    trow = lax.broadcasted_iota(jnp.int32, (c, MIX_W), 0)
    coef = coef_ref[...]

    def body(j, carry):
        r0 = pl.multiple_of(j * c, c)
        fz = f_ref[pl.ds(r0, c), :]
        qf = _silu(q_ref[pl.ds(r0, c), :])
        f = lb + (1.0 - lb) * _sigmoid(fz)
        lf = jnp.log(jnp.maximum(f, LOG_FLOOR))
        kf = (1.0 - lb) * _sigmoid(-fz)
        v = i_ref[pl.ds(r0, c), :]
        vb = v.astype(BF16)

        sums = _dot_exact_lhs(coef, lf)
        pc = sums[0:c]
        ptot = pc[c - 1:c, :]

        st = st_ref[...]
        qd = (qf * jnp.exp(pc)).astype(BF16)
        o = _dot_nt(qd, st.astype(BF16))
        kd = (kf * jnp.exp(ptot - pc)).astype(BF16)
        st_ref[...] = st * jnp.exp(ptot) + jnp.where(same_head, _dot_tn(vb, kd), 0.0)

        diag = _dot_exact_rhs(qf * kf, head_ones)
        o = o + diag * v

        qts, kts = [], []
        for li, g in enumerate(HG_LEVELS):
            cg = sums[(li + 1) * c:(li + 2) * c]
            eg = jnp.exp(cg)
            right = (trow // g) % 2 == 1
            qts.append(jnp.where(right, qf * eg, 0.0))
            kts.append(jnp.where(right, 0.0, kf * eg).astype(BF16))
        for h in range(N_HEADS):
            a = jnp.zeros((c, c), F32)
            for li, g in enumerate(HG_LEVELS):
                qm = jnp.where(lane_head == h, qts[li], 0.0).astype(BF16)
                sc = _dot_nt(qm, kts[li])
                if 2 * g < c:
                    sc = jnp.where((row // (2 * g)) == (colm // (2 * g)), sc, 0.0)
                a = a + sc
            o = o + jnp.where(lane_head == h, _dot(a.astype(BF16), vb), 0.0)

        ms = _dot_exact_rhs(o * o, head_ones) * (1.0 / HEAD_DIM)
        o = o * lax.rsqrt(ms + EPS) * ng_ref[...]
        o = o * _silu(g_ref[pl.ds(r0, c), :])
        o_ref[pl.ds(r0, c), :] = o.astype(o_ref.dtype)
        return carry

    lax.fori_loop(0, s_len // c, body, 0)


def _hgrn2(z3, layer, lb_logits, norm_g):
    b, s, _ = z3.shape
    col_spec = lambda c: pl.BlockSpec((None, s, MIX_W), lambda i, c=c: (i, 0, c))
    coef = jnp.asarray(_hgrn_coef_matrix(), dtype=BF16)
    return pl.pallas_call(
        functools.partial(_hgrn_kernel, layer),
        grid=(b,),
        in_specs=[col_spec(3), col_spec(4), col_spec(5), col_spec(6),
                  pl.BlockSpec((DEPTH, MIX_W), lambda i: (0, 0)),
                  pl.BlockSpec((1, MIX_W), lambda i: (0, 0)),
                  pl.BlockSpec(coef.shape, lambda i: (0, 0))],
        out_specs=pl.BlockSpec((None, s, MIX_W), lambda i: (i, 0, 0)),
        out_shape=jax.ShapeDtypeStruct((b, s, MIX_W), BF16),
        scratch_shapes=[pltpu.VMEM((MIX_W, MIX_W), F32)],
        compiler_params=_compiler_params(1),
        name="hgrn2",
    )(z3, z3, z3, z3, lb_logits, norm_g.reshape(1, MIX_W), coef)


def _gmlp_kernel(u_ref, v_ref, ng_ref, ws_ref, bias_ref, o_ref):
    s_len = u_ref.shape[0]
    prow = lax.broadcasted_iota(jnp.int32, (GM_BLOCK, GM_BLOCK), 0)
    pcol = lax.broadcasted_iota(jnp.int32, (GM_BLOCK, GM_BLOCK), 1)
    lane_head = _head_of_lane((GM_BLOCK, MIX_W), 1)
    ws = [jnp.where(pcol <= prow, ws_ref[g], 0.0).astype(BF16) for g in range(N_HEADS)]

    def body(j, carry):
        r0 = pl.multiple_of(j * GM_BLOCK, GM_BLOCK)
        vn = _rms(_gelu(v_ref[pl.ds(r0, GM_BLOCK), :]), ng_ref[...]).astype(BF16)
        mixed = bias_ref[...]
        for g in range(N_HEADS):
            mixed = mixed + jnp.where(lane_head == g, _dot(ws[g], vn), 0.0)
        u = _gelu(u_ref[pl.ds(r0, GM_BLOCK), :])
        o_ref[pl.ds(r0, GM_BLOCK), :] = (u * mixed).astype(o_ref.dtype)
        return carry

    lax.fori_loop(0, s_len // GM_BLOCK, body, 0)


def _gmlp(z3, norm_g, ws, bs):
    b, s, _ = z3.shape
    col_spec = lambda c: pl.BlockSpec((None, s, MIX_W), lambda i, c=c: (i, 0, c))
    bias = jnp.repeat(bs.T, HEAD_DIM, axis=1)
    return pl.pallas_call(
        _gmlp_kernel,
        grid=(b,),
        in_specs=[col_spec(7), col_spec(8),
                  pl.BlockSpec((1, MIX_W), lambda i: (0, 0)),
                  pl.BlockSpec((N_HEADS, GM_BLOCK, GM_BLOCK), lambda i: (0, 0, 0)),
                  pl.BlockSpec((GM_BLOCK, MIX_W), lambda i: (0, 0))],
        out_specs=pl.BlockSpec((None, s, MIX_W), lambda i: (i, 0, 0)),
        out_shape=jax.ShapeDtypeStruct((b, s, MIX_W), BF16),
        compiler_params=_compiler_params(1),
        name="gmlp",
    )(z3, z3, norm_g.reshape(1, MIX_W), ws, bias)


def _lru_kernel(x_ref, gate_ref, cw_ref, cb_ref, wa_ref, ba_ref, wx_ref, bx_ref, lam_ref,
                o_ref, a_ref, b_ref):
    s_len = x_ref.shape[0]
    x = x_ref[...]
    t = lax.broadcasted_iota(jnp.int32, (s_len, MIX_W), 0)
    xc = cb_ref[...] + x * cw_ref[CONV_W - 1:CONV_W, :]
    for d in range(1, CONV_W):
        xs = jnp.where(t >= d, pltpu.roll(x, d, 0), 0.0)
        xc = xc + xs * cw_ref[CONV_W - 1 - d:CONV_W - d, :]
    xcb = xc.astype(BF16)
    r = _sigmoid(_dot(xcb, wa_ref[...]) + ba_ref[...])
    ig = _sigmoid(_dot(xcb, wx_ref[...]) + bx_ref[...])
    nl = -lam_ref[...]
    softplus = jnp.maximum(nl, 0.0) + jnp.log(1.0 + jnp.exp(-jnp.abs(nl)))
    log_a = -LRU_C * r * softplus
    mult = jnp.sqrt(jnp.maximum(1.0 - jnp.exp(2.0 * log_a), 0.0))
    mult = jnp.where(t == 0, 1.0, mult)
    a_ref[...] = jnp.exp(log_a)
    b_ref[...] = mult * (ig * xc)
    d = 1
    while d < s_len:
        a = a_ref[...]
        b = b_ref[...]
        ok = t >= d
        a_prev = jnp.where(ok, pltpu.roll(a, d, 0), 1.0)
        b_prev = jnp.where(ok, pltpu.roll(b, d, 0), 0.0)
        b_ref[...] = a * b_prev + b
        a_ref[...] = a * a_prev
        d *= 2
    o_ref[...] = (b_ref[...] * _gelu(gate_ref[...])).astype(o_ref.dtype)


def _block_diag(w):
    out = jnp.zeros((MIX_W, MIX_W), w.dtype)
    for h in range(N_HEADS):
        out = out.at[h * HEAD_DIM:(h + 1) * HEAD_DIM, h * HEAD_DIM:(h + 1) * HEAD_DIM].set(w[h])
    return out


def _rg_lru(z3, conv_w, conv_b, wa, ba, wx, bx, lam):
    b, s, _ = z3.shape
    col_spec = lambda c: pl.BlockSpec((None, s, MIX_W), lambda i, c=c: (i, 0, c))
    row_spec = pl.BlockSpec((1, MIX_W), lambda i: (0, 0))
    mat_spec = pl.BlockSpec((MIX_W, MIX_W), lambda i: (0, 0))
    return pl.pallas_call(
        _lru_kernel,
        grid=(b,),
        in_specs=[col_spec(9), col_spec(10),
                  pl.BlockSpec((CONV_W, MIX_W), lambda i: (0, 0)), row_spec,
                  mat_spec, row_spec, mat_spec, row_spec, row_spec],
        out_specs=pl.BlockSpec((None, s, MIX_W), lambda i: (i, 0, 0)),
        out_shape=jax.ShapeDtypeStruct((b, s, MIX_W), BF16),
        scratch_shapes=[pltpu.VMEM((s, MIX_W), F32), pltpu.VMEM((s, MIX_W), F32)],
        compiler_params=_compiler_params(1),
        name="rg_lru",
    )(z3, z3, conv_w, conv_b.reshape(1, MIX_W),
      _block_diag(wa).astype(BF16), ba.reshape(1, MIX_W),
      _block_diag(wx).astype(BF16), bx.reshape(1, MIX_W), lam.reshape(1, MIX_W))


def _mix_out_kernel(x_ref, oa_ref, ob_ref, oc_ref, od_ref, gpre_ref, gpost_ref,
                    wg_ref, wb_ref, wo_ref, out_ref):
    x = x_ref[...]
    h = _rms(x, gpre_ref[...]).astype(BF16)
    merged = jnp.zeros(x.shape, F32)
    for n, o_ref in enumerate((oa_ref, ob_ref, oc_ref, od_ref)):
        gate = _dot(h, wg_ref[:, n * D_MODEL:(n + 1) * D_MODEL])
        proj = _dot(o_ref[...], wb_ref[n])
        merged = merged + _sigmoid(gate) * proj
    y = _dot(merged.astype(BF16), wo_ref[...])
    out_ref[...] = x + _rms(y, gpost_ref[...])


def _mix_out(x2, outs, g_pre, g_post, w_gate, w_branch, w_out):
    t = x2.shape[0]
    tok = lambda w: pl.BlockSpec((TOKEN_TILE, w), lambda i: (i, 0))
    row_spec = pl.BlockSpec((1, D_MODEL), lambda i: (0, 0))
    return pl.pallas_call(
        _mix_out_kernel,
        grid=(t // TOKEN_TILE,),
        in_specs=[tok(D_MODEL), tok(MIX_W), tok(MIX_W), tok(MIX_W), tok(MIX_W),
                  row_spec, row_spec,
                  pl.BlockSpec((D_MODEL, N_BRANCH * D_MODEL), lambda i: (0, 0)),
                  pl.BlockSpec((N_BRANCH, MIX_W, D_MODEL), lambda i: (0, 0, 0)),
                  pl.BlockSpec((D_MODEL, D_MODEL), lambda i: (0, 0))],
        out_specs=tok(D_MODEL),
        out_shape=jax.ShapeDtypeStruct((t, D_MODEL), F32),
        compiler_params=_compiler_params(1),
        name="mix_out",
    )(x2, *outs, g_pre, g_post, w_gate, w_branch, w_out)


def _ffn_kernel(x_ref, gpre_ref, gpost_ref, w1_ref, w2_ref, out_ref):
    x = x_ref[...]
    h = _rms(x, gpre_ref[...]).astype(BF16)
    gu = _dot(h, w1_ref[...])
    act = (_silu(gu[:, :FFN_HIDDEN]) * gu[:, FFN_HIDDEN:]).astype(BF16)
    y = _dot(act, w2_ref[...])
    out_ref[...] = x + _rms(y, gpost_ref[...])


def _ffn(x2, g_pre, g_post, w1, w2):
    t = x2.shape[0]
    tile = TOKEN_TILE // 2
    tok = pl.BlockSpec((tile, D_MODEL), lambda i: (i, 0))
    row_spec = pl.BlockSpec((1, D_MODEL), lambda i: (0, 0))
    return pl.pallas_call(
        _ffn_kernel,
        grid=(t // tile,),
        in_specs=[tok, row_spec, row_spec,
                  pl.BlockSpec((D_MODEL, 2 * FFN_HIDDEN), lambda i: (0, 0)),
                  pl.BlockSpec((FFN_HIDDEN, D_MODEL), lambda i: (0, 0))],
        out_specs=tok,
        out_shape=jax.ShapeDtypeStruct((t, D_MODEL), F32),
        compiler_params=_compiler_params(1),
        name="ffn",
    )(x2, g_pre, g_post, w1, w2)


def kernel(x, norm_mix_pre, norm_mix_post, norm_ffn_pre, norm_ffn_post, w_in, attn_rel_bias, hgrn_lb_logits, hgrn_norm_g, gmlp_norm_g, gmlp_ws, gmlp_bs, lru_conv_w, lru_conv_b, lru_wa, lru_ba, lru_wx, lru_bx, lru_lambda, w_branch, w_out, w_ffn_in, w_ffn_out):
    b, s, d = x.shape
    x2 = x.reshape(b * s, d)
    w_in_b = w_in.astype(BF16)
    w_branch_b = w_branch.astype(BF16)
    w_out_b = w_out.astype(BF16)
    w1_b = w_ffn_in.astype(BF16)
    w2_b = w_ffn_out.astype(BF16)
    for l in range(DEPTH):
        z = _in_proj(x2, norm_mix_pre[l].reshape(1, d), w_in_b[l, :, :N_MIX_COLS])
        z3 = z.reshape(b, s, N_MIX_COLS)
        o_a = _attention(z3, attn_rel_bias[l])
        o_b = _hgrn2(z3, l, hgrn_lb_logits, hgrn_norm_g[l])
        o_c = _gmlp(z3, gmlp_norm_g[l], gmlp_ws[l], gmlp_bs[l])
        o_d = _rg_lru(z3, lru_conv_w[l], lru_conv_b[l], lru_wa[l], lru_ba[l],
                      lru_wx[l], lru_bx[l], lru_lambda[l])
        outs = [o.reshape(b * s, MIX_W) for o in (o_a, o_b, o_c, o_d)]
        x2 = _mix_out(x2, outs, norm_mix_pre[l].reshape(1, d), norm_mix_post[l].reshape(1, d),
                      w_in_b[l, :, N_MIX_COLS:], w_branch_b[l], w_out_b[l])
        x2 = _ffn(x2, norm_ffn_pre[l].reshape(1, d), norm_ffn_post[l].reshape(1, d),
                  w1_b[l], w2_b[l])
    return x2.reshape(b, s, d)
```

```python
import functools

import numpy as np
import jax
import jax.numpy as jnp
from jax import lax
from jax.experimental import pallas as pl
from jax.experimental.pallas import tpu as pltpu

D_MODEL = 1024
DEPTH = 4
CHUNK = 64
EPS = 1e-6
N_HEADS = 4
HEAD_DIM = 64
MIX_W = N_HEADS * HEAD_DIM
N_BRANCH = 4
ATT_LEFT_CHUNKS = 8
REL_MAX = 256
NEG_BIG = -1e30
LOG_FLOOR = 1e-30
GM_BLOCK = 128
CONV_W = 4
LRU_C = 8.0
FFN_HIDDEN = 2816
N_MIX_COLS = 11 * MIX_W

F32 = jnp.float32
BF16 = jnp.bfloat16

VMEM_LIMIT_BYTES = 56 * 1024 * 1024
TOKEN_TILE = 512
ATT_QB = 256
ATT_PAD = ATT_LEFT_CHUNKS * CHUNK
ATT_KB = ATT_PAD + ATT_QB
ATT_ROLL_W = 1024
HG_CHUNK = 256
HG_LEVELS = (1, 2, 4, 8, 16, 32, 64, 128)


def _compiler_params(n_grid):
    return pltpu.CompilerParams(
        dimension_semantics=("arbitrary",) * n_grid,
        vmem_limit_bytes=VMEM_LIMIT_BYTES)


def _rms(x, g):
    return x * lax.rsqrt(jnp.mean(x * x, axis=-1, keepdims=True) + EPS) * g


def _gelu(x):
    return 0.5 * x * (1.0 + jnp.tanh(0.7978845608028654 * (x + 0.044715 * x * x * x)))


def _sigmoid(x):
    return 1.0 / (1.0 + jnp.exp(-x))


def _silu(x):
    return x * _sigmoid(x)


def _dot(a, b):
    return jnp.dot(a, b, preferred_element_type=F32)


def _dot_nt(a, b):
    return lax.dot_general(a, b, (((1,), (1,)), ((), ())), preferred_element_type=F32)


def _dot_tn(a, b):
    return lax.dot_general(a, b, (((0,), (0,)), ((), ())), preferred_element_type=F32)


def _split2(x):
    hi = x.astype(BF16)
    lo = (x - hi.astype(F32)).astype(BF16)
    return hi, lo


def _head_of_lane(shape, dim):
    return lax.broadcasted_iota(jnp.int32, shape, dim) // HEAD_DIM


def _layer_spec(shape, layer):
    zeros = (0,) * len(shape)
    return pl.BlockSpec((None,) + tuple(shape), lambda i: (layer,) + zeros)


def _in_proj_kernel(x_ref, g_ref, w_ref, z_ref):
    h = _rms(x_ref[...], g_ref[...]).astype(BF16)
    z_ref[...] = _dot(h, w_ref[...])


def _in_proj(x2, layer, g, w_mix):
    t = x2.shape[0]
    return pl.pallas_call(
        _in_proj_kernel,
        grid=(t // TOKEN_TILE,),
        in_specs=[
            pl.BlockSpec((TOKEN_TILE, D_MODEL), lambda i: (i, 0)),
            _layer_spec((1, D_MODEL), layer),
            _layer_spec((D_MODEL, N_MIX_COLS), layer),
        ],
        out_specs=pl.BlockSpec((TOKEN_TILE, N_MIX_COLS), lambda i: (i, 0)),
        out_shape=jax.ShapeDtypeStruct((t, N_MIX_COLS), F32),
        compiler_params=_compiler_params(1),
        name="in_proj",
    )(x2, g, w_mix)


def _attn_kernel(q_ref, k_ref, v_ref, brow_ref, o_ref, kpad, vpad, bias_scr):
    s_len = q_ref.shape[0]

    @pl.when(pl.program_id(0) == 0)
    def _():
        kpad[0:ATT_PAD, :] = jnp.zeros((ATT_PAD, MIX_W), BF16)
        vpad[0:ATT_PAD, :] = jnp.zeros((ATT_PAD, MIX_W), BF16)
        qi = lax.broadcasted_iota(jnp.int32, (ATT_QB, ATT_KB), 0) // CHUNK
        kc = lax.broadcasted_iota(jnp.int32, (ATT_QB, ATT_KB), 1) // CHUNK
        band = (kc >= qi) & (kc <= qi + ATT_LEFT_CHUNKS)
        for h in range(N_HEADS):
            rows = jnp.broadcast_to(brow_ref[h:h + 1, :], (ATT_QB, ATT_ROLL_W))
            tile = pltpu.roll(rows, ATT_ROLL_W - ATT_QB, 1, stride=1, stride_axis=0)
            bias_scr[h] = jnp.where(band, tile[:, :ATT_KB], NEG_BIG)

    kpad[ATT_PAD:, :] = k_ref[...].astype(BF16)
    vpad[ATT_PAD:, :] = v_ref[...].astype(BF16)
    lane_head = _head_of_lane((ATT_QB, MIX_W), 1)
    col = lax.broadcasted_iota(jnp.int32, (ATT_QB, ATT_KB), 1)

    def body(j, carry):
        r0 = pl.multiple_of(j * ATT_QB, ATT_QB)
        q = q_ref[pl.ds(r0, ATT_QB), :] * (HEAD_DIM ** -0.5)
        kb = kpad[pl.ds(r0, ATT_KB), :]
        vb = vpad[pl.ds(r0, ATT_KB), :]
        in_seq = col >= ATT_PAD - r0
        acc = jnp.zeros((ATT_QB, MIX_W), F32)
        for h in range(N_HEADS):
            qm = jnp.where(lane_head == h, q, 0.0).astype(BF16)
            s = _dot_nt(qm, kb) + bias_scr[h]
            s = jnp.where(in_seq, s, NEG_BIG)
            m = jnp.max(s, axis=-1, keepdims=True)
            p = jnp.exp(s - m)
            l = jnp.sum(p, axis=-1, keepdims=True)
            o = _dot(p.astype(BF16), vb)
            acc = acc + jnp.where(lane_head == h, o * (1.0 / l), 0.0)
        o_ref[pl.ds(r0, ATT_QB), :] = acc.astype(o_ref.dtype)
        return carry

    lax.fori_loop(0, s_len // ATT_QB, body, 0)


def _attn_bias_rows(rel_bias):
    dist = ATT_PAD + ATT_QB - np.arange(ATT_ROLL_W)
    idx = np.clip(dist, -(CHUNK - 1), REL_MAX) + (CHUNK - 1)
    return rel_bias.astype(F32)[:, :, idx]


def _attention(z3, layer, bias_rows):
    b, s, _ = z3.shape
    col_spec = lambda c: pl.BlockSpec((None, s, MIX_W), lambda i, c=c: (i, 0, c))
    return pl.pallas_call(
        _attn_kernel,
        grid=(b,),
        in_specs=[col_spec(0), col_spec(1), col_spec(2),
                  _layer_spec((N_HEADS, ATT_ROLL_W), layer)],
        out_specs=pl.BlockSpec((None, s, MIX_W), lambda i: (i, 0, 0)),
        out_shape=jax.ShapeDtypeStruct((b, s, MIX_W), BF16),
        scratch_shapes=[pltpu.VMEM((ATT_PAD + s, MIX_W), BF16),
                        pltpu.VMEM((ATT_PAD + s, MIX_W), BF16),
                        pltpu.VMEM((N_HEADS, ATT_QB, ATT_KB), F32)],
        compiler_params=_compiler_params(1),
        name="attention",
    )(z3, z3, z3, bias_rows)


def _hgrn_coef_matrix():
    c = HG_CHUNK
    t = np.arange(c)[:, None]
    r = np.arange(c)[None, :]
    mats = [(r <= t)]
    for g in HG_LEVELS:
        bnd = (t // (2 * g)) * (2 * g) + g - 1
        right = (t // g) % 2 == 1
        m_right = (r > bnd) & (r <= t)
        m_left = (r > t) & (r <= bnd)
        mats.append(np.where(right, m_right, m_left))
    return np.concatenate(mats, axis=0).astype(np.float32)


def _hgrn_kernel(layer, q_ref, f_ref, i_ref, g_ref, lbl_ref, ng_ref, coef_ref, o_ref, st_ref):
    s_len = q_ref.shape[0]
    c = HG_CHUNK
    logits = lbl_ref[...]
    e = jnp.exp(logits - jnp.max(logits, axis=0, keepdims=True))
    p = e / jnp.sum(e, axis=0, keepdims=True)
    lb = jnp.zeros((1, MIX_W), F32)
    for l in range(1, layer + 1):
        lb = lb + p[l:l + 1, :]

    st_ref[...] = jnp.zeros((MIX_W, MIX_W), F32)
    row = lax.broadcasted_iota(jnp.int32, (c, c), 0)
    colm = lax.broadcasted_iota(jnp.int32, (c, c), 1)
    lane_head = _head_of_lane((c, MIX_W), 1)
    same_head = (_head_of_lane((MIX_W, MIX_W), 0) == _head_of_lane((MIX_W, MIX_W), 1))
    head_ones = jnp.where(same_head, 1.0, 0.0).astype(BF16)
    trow = lax.broadcasted_iota(jnp.int32, (c, MIX_W), 0)

    def body(j, carry):
        r0 = pl.multiple_of(j * c, c)
        fz = f_ref[pl.ds(r0, c), :]
        qf = _silu(q_ref[pl.ds(r0, c), :])
        f = lb + (1.0 - lb) * _sigmoid(fz)
        lf = jnp.log(jnp.maximum(f, LOG_FLOOR))
        kf = (1.0 - lb) * _sigmoid(-fz)
        v = i_ref[pl.ds(r0, c), :]
        vb = v.astype(BF16)

        lf_hi, lf_lo = _split2(lf)
        pc = _dot(coef_ref[0:c, :], lf_hi) + _dot(coef_ref[0:c, :], lf_lo)
        sums = _dot(coef_ref[c:, :], lf_hi)
        ptot = pc[c - 1:c, :]

        st = st_ref[...]
        qd = (qf * jnp.exp(pc)).astype(BF16)
        o = _dot_nt(qd, st.astype(BF16))
        kd = (kf * jnp.exp(ptot - pc)).astype(BF16)
        st_ref[...] = st * jnp.exp(ptot) + jnp.where(same_head, _dot_tn(vb, kd), 0.0)

        o = o + _dot((qf * kf).astype(BF16), head_ones) * v

        qts, kts = [], []
        for li, g in enumerate(HG_LEVELS):
            eg = jnp.exp(sums[li * c:(li + 1) * c])
            right = (trow // g) % 2 == 1
            qts.append(jnp.where(right, qf * eg, 0.0))
            kts.append(jnp.where(right, 0.0, kf * eg).astype(BF16))
        for h in range(N_HEADS):
            a = jnp.zeros((c, c), F32)
            for li, g in enumerate(HG_LEVELS):
                qm = jnp.where(lane_head == h, qts[li], 0.0).astype(BF16)
                sc = _dot_nt(qm, kts[li])
                if 2 * g < c:
                    sc = jnp.where((row // (2 * g)) == (colm // (2 * g)), sc, 0.0)
                a = a + sc
            o = o + jnp.where(lane_head == h, _dot(a.astype(BF16), vb), 0.0)

        ms = _dot((o * o).astype(BF16), head_ones) * (1.0 / HEAD_DIM)
        o = o * lax.rsqrt(ms + EPS) * ng_ref[...]
        o = o * _silu(g_ref[pl.ds(r0, c), :])
        o_ref[pl.ds(r0, c), :] = o.astype(o_ref.dtype)
        return carry

    lax.fori_loop(0, s_len // c, body, 0, unroll=2)


def _hgrn2(z3, layer, lb_logits, norm_g, coef):
    b, s, _ = z3.shape
    col_spec = lambda c: pl.BlockSpec((None, s, MIX_W), lambda i, c=c: (i, 0, c))
    return pl.pallas_call(
        functools.partial(_hgrn_kernel, layer),
        grid=(b,),
        in_specs=[col_spec(3), col_spec(4), col_spec(5), col_spec(6),
                  pl.BlockSpec((DEPTH, MIX_W), lambda i: (0, 0)),
                  _layer_spec((1, MIX_W), layer),
                  pl.BlockSpec(coef.shape, lambda i: (0, 0))],
        out_specs=pl.BlockSpec((None, s, MIX_W), lambda i: (i, 0, 0)),
        out_shape=jax.ShapeDtypeStruct((b, s, MIX_W), BF16),
        scratch_shapes=[pltpu.VMEM((MIX_W, MIX_W), F32)],
        compiler_params=_compiler_params(1),
        name="hgrn2",
    )(z3, z3, z3, z3, lb_logits, norm_g, coef)


def _gmlp_kernel(u_ref, v_ref, ng_ref, ws_ref, bias_ref, o_ref):
    s_len = u_ref.shape[0]
    prow = lax.broadcasted_iota(jnp.int32, (GM_BLOCK, GM_BLOCK), 0)
    pcol = lax.broadcasted_iota(jnp.int32, (GM_BLOCK, GM_BLOCK), 1)
    lane_head = _head_of_lane((GM_BLOCK, MIX_W), 1)
    ws = [jnp.where(pcol <= prow, ws_ref[g], 0.0).astype(BF16) for g in range(N_HEADS)]

    def body(j, carry):
        r0 = pl.multiple_of(j * GM_BLOCK, GM_BLOCK)
        vn = _rms(_gelu(v_ref[pl.ds(r0, GM_BLOCK), :]), ng_ref[...]).astype(BF16)
        mixed = bias_ref[...]
        for g in range(N_HEADS):
            mixed = mixed + jnp.where(lane_head == g, _dot(ws[g], vn), 0.0)
        u = _gelu(u_ref[pl.ds(r0, GM_BLOCK), :])
        o_ref[pl.ds(r0, GM_BLOCK), :] = (u * mixed).astype(o_ref.dtype)
        return carry

    lax.fori_loop(0, s_len // GM_BLOCK, body, 0, unroll=2)


def _gmlp(z3, layer, norm_g, ws, bias):
    b, s, _ = z3.shape
    col_spec = lambda c: pl.BlockSpec((None, s, MIX_W), lambda i, c=c: (i, 0, c))
    return pl.pallas_call(
        _gmlp_kernel,
        grid=(b,),
        in_specs=[col_spec(7), col_spec(8),
                  _layer_spec((1, MIX_W), layer),
                  _layer_spec((N_HEADS, GM_BLOCK, GM_BLOCK), layer),
                  _layer_spec((GM_BLOCK, MIX_W), layer)],
        out_specs=pl.BlockSpec((None, s, MIX_W), lambda i: (i, 0, 0)),
        out_shape=jax.ShapeDtypeStruct((b, s, MIX_W), BF16),
        compiler_params=_compiler_params(1),
        name="gmlp",
    )(z3, z3, norm_g, ws, bias)


LRU_ROW_CONV_B, LRU_ROW_BA, LRU_ROW_BX, LRU_ROW_LAM = 4, 5, 6, 7


def _lru_kernel(x_ref, gate_ref, vec_ref, w_ref, o_ref, a_ref, b_ref):
    s_len = x_ref.shape[0]
    vrow = lambda k: vec_ref[k:k + 1, :]
    x = x_ref[...]
    t = lax.broadcasted_iota(jnp.int32, (s_len, MIX_W), 0)
    xc = vrow(LRU_ROW_CONV_B) + x * vrow(CONV_W - 1)
    for d in range(1, CONV_W):
        xs = jnp.where(t >= d, pltpu.roll(x, d, 0), 0.0)
        xc = xc + xs * vrow(CONV_W - 1 - d)
    xcb = xc.astype(BF16)
    r = _sigmoid(_dot(xcb, w_ref[0]) + vrow(LRU_ROW_BA))
    ig = _sigmoid(_dot(xcb, w_ref[1]) + vrow(LRU_ROW_BX))
    nl = -vrow(LRU_ROW_LAM)
    softplus = jnp.maximum(nl, 0.0) + jnp.log(1.0 + jnp.exp(-jnp.abs(nl)))
    log_a = -LRU_C * r * softplus
    mult = jnp.sqrt(jnp.maximum(1.0 - jnp.exp(2.0 * log_a), 0.0))
    mult = jnp.where(t == 0, 1.0, mult)
    a_ref[...] = jnp.exp(log_a)
    b_ref[...] = mult * (ig * xc)
    d = 1
    while d < s_len:
        a = a_ref[...]
        b = b_ref[...]
        ok = t >= d
        a_prev = jnp.where(ok, pltpu.roll(a, d, 0), 1.0)
        b_prev = jnp.where(ok, pltpu.roll(b, d, 0), 0.0)
        b_ref[...] = a * b_prev + b
        a_ref[...] = a * a_prev
        d *= 2
    o_ref[...] = (b_ref[...] * _gelu(gate_ref[...])).astype(o_ref.dtype)


def _block_diag(w):
    eye = jnp.eye(N_HEADS, dtype=w.dtype)
    bd = jnp.einsum('...hij,hg->...higj', w, eye)
    return bd.reshape(w.shape[:-3] + (MIX_W, MIX_W))


def _rg_lru(z3, layer, vec, w_bd):
    b, s, _ = z3.shape
    col_spec = lambda c: pl.BlockSpec((None, s, MIX_W), lambda i, c=c: (i, 0, c))
    return pl.pallas_call(
        _lru_kernel,
        grid=(b,),
        in_specs=[col_spec(9), col_spec(10),
                  _layer_spec((8, MIX_W), layer),
                  _layer_spec((2, MIX_W, MIX_W), layer)],
        out_specs=pl.BlockSpec((None, s, MIX_W), lambda i: (i, 0, 0)),
        out_shape=jax.ShapeDtypeStruct((b, s, MIX_W), BF16),
        scratch_shapes=[pltpu.VMEM((s, MIX_W), F32), pltpu.VMEM((s, MIX_W), F32)],
        compiler_params=_compiler_params(1),
        name="rg_lru",
    )(z3, z3, vec, w_bd)


def _mix_out_kernel(x_ref, oa_ref, ob_ref, oc_ref, od_ref, gpre_ref, gpost_ref,
                    wg_ref, wb_ref, wo_ref, out_ref):
    x = x_ref[...]
    h = _rms(x, gpre_ref[...]).astype(BF16)
    merged = jnp.zeros(x.shape, F32)
    for n, o_ref in enumerate((oa_ref, ob_ref, oc_ref, od_ref)):
        gate = _dot(h, wg_ref[:, n * D_MODEL:(n + 1) * D_MODEL])
        proj = _dot(o_ref[...], wb_ref[n])
        merged = merged + _sigmoid(gate) * proj
    y = _dot(merged.astype(BF16), wo_ref[...])
    out_ref[...] = x + _rms(y, gpost_ref[...])


def _mix_out(x2, outs, layer, g_pre, g_post, w_gate, w_branch, w_out):
    t = x2.shape[0]
    tok = lambda w: pl.BlockSpec((TOKEN_TILE, w), lambda i: (i, 0))
    return pl.pallas_call(
        _mix_out_kernel,
        grid=(t // TOKEN_TILE,),
        in_specs=[tok(D_MODEL), tok(MIX_W), tok(MIX_W), tok(MIX_W), tok(MIX_W),
                  _layer_spec((1, D_MODEL), layer), _layer_spec((1, D_MODEL), layer),
                  _layer_spec((D_MODEL, N_BRANCH * D_MODEL), layer),
                  _layer_spec((N_BRANCH, MIX_W, D_MODEL), layer),
                  _layer_spec((D_MODEL, D_MODEL), layer)],
        out_specs=tok(D_MODEL),
        out_shape=jax.ShapeDtypeStruct((t, D_MODEL), F32),
        compiler_params=_compiler_params(1),
        name="mix_out",
    )(x2, *outs, g_pre, g_post, w_gate, w_branch, w_out)


def _ffn_kernel(x_ref, gpre_ref, gpost_ref, w1_ref, w2_ref, out_ref):
    x = x_ref[...]
    h = _rms(x, gpre_ref[...]).astype(BF16)
    gu = _dot(h, w1_ref[...])
    act = (_silu(gu[:, :FFN_HIDDEN]) * gu[:, FFN_HIDDEN:]).astype(BF16)
    y = _dot(act, w2_ref[...])
    out_ref[...] = x + _rms(y, gpost_ref[...])


def _ffn(x2, layer, g_pre, g_post, w1, w2):
    t = x2.shape[0]
    tile = TOKEN_TILE // 2
    tok = pl.BlockSpec((tile, D_MODEL), lambda i: (i, 0))
    return pl.pallas_call(
        _ffn_kernel,
        grid=(t // tile,),
        in_specs=[tok, _layer_spec((1, D_MODEL), layer), _layer_spec((1, D_MODEL), layer),
                  _layer_spec((D_MODEL, 2 * FFN_HIDDEN), layer),
                  _layer_spec((FFN_HIDDEN, D_MODEL), layer)],
        out_specs=tok,
        out_shape=jax.ShapeDtypeStruct((t, D_MODEL), F32),
        compiler_params=_compiler_params(1),
        name="ffn",
    )(x2, g_pre, g_post, w1, w2)


def kernel(x, norm_mix_pre, norm_mix_post, norm_ffn_pre, norm_ffn_post, w_in, attn_rel_bias, hgrn_lb_logits, hgrn_norm_g, gmlp_norm_g, gmlp_ws, gmlp_bs, lru_conv_w, lru_conv_b, lru_wa, lru_ba, lru_wx, lru_bx, lru_lambda, w_branch, w_out, w_ffn_in, w_ffn_out):
    b, s, d = x.shape
    x2 = x.reshape(b * s, d)
    w_mix_b = w_in[:, :, :N_MIX_COLS].astype(BF16)
    w_gate_b = w_in[:, :, N_MIX_COLS:].astype(BF16)
    w_branch_b = w_branch.astype(BF16)
    w_out_b = w_out.astype(BF16)
    w1_b = w_ffn_in.astype(BF16)
    w2_b = w_ffn_out.astype(BF16)
    row = lambda p: p.reshape(DEPTH, 1, p.shape[-1])
    g_mix_pre, g_mix_post = row(norm_mix_pre), row(norm_mix_post)
    g_ffn_pre, g_ffn_post = row(norm_ffn_pre), row(norm_ffn_post)
    bias_rows = _attn_bias_rows(attn_rel_bias)
    hg_coef = jnp.asarray(_hgrn_coef_matrix(), dtype=BF16)
    hg_norm = row(hgrn_norm_g)
    gm_norm = row(gmlp_norm_g)
    gm_bias = jnp.repeat(jnp.swapaxes(gmlp_bs, 1, 2), HEAD_DIM, axis=2)
    lru_vec = jnp.concatenate(
        [lru_conv_w, lru_conv_b[:, None], lru_ba[:, None], lru_bx[:, None], lru_lambda[:, None]],
        axis=1)
    lru_w = _block_diag(jnp.stack([lru_wa, lru_wx], axis=1)).astype(BF16)
    for l in range(DEPTH):
        z = _in_proj(x2, l, g_mix_pre, w_mix_b)
        z3 = z.reshape(b, s, N_MIX_COLS)
        o_a = _attention(z3, l, bias_rows)
        o_b = _hgrn2(z3, l, hgrn_lb_logits, hg_norm, hg_coef)
        o_c = _gmlp(z3, l, gm_norm, gmlp_ws, gm_bias)
        o_d = _rg_lru(z3, l, lru_vec, lru_w)
        outs = [o.reshape(b * s, MIX_W) for o in (o_a, o_b, o_c, o_d)]
        x2 = _mix_out(x2, outs, l, g_mix_pre, g_mix_post, w_gate_b, w_branch_b, w_out_b)
        x2 = _ffn(x2, l, g_ffn_pre, g_ffn_post, w1_b, w2_b)
    return x2.reshape(b, s, d)
```

```python
import functools

import numpy as np
import jax
import jax.numpy as jnp
from jax import lax
from jax.experimental import pallas as pl
from jax.experimental.pallas import tpu as pltpu

D_MODEL = 1024
DEPTH = 4
CHUNK = 64
EPS = 1e-6
N_HEADS = 4
HEAD_DIM = 64
MIX_W = N_HEADS * HEAD_DIM
N_BRANCH = 4
ATT_LEFT_CHUNKS = 8
REL_MAX = 256
NEG_BIG = -1e30
LOG_FLOOR = 1e-30
GM_BLOCK = 128
CONV_W = 4
LRU_C = 8.0
FFN_HIDDEN = 2816
N_MIX_COLS = 11 * MIX_W
N_ATT_COLS = 3 * MIX_W
LOG2_E = 1.4426950408889634
ATT_Q_SCALE = HEAD_DIM ** -0.5 * LOG2_E

F32 = jnp.float32
BF16 = jnp.bfloat16

VMEM_LIMIT_BYTES = 56 * 1024 * 1024
TOKEN_TILE = 512
ATT_QB = 256
ATT_PAD = ATT_LEFT_CHUNKS * CHUNK
ATT_KB = ATT_PAD + ATT_QB
ATT_ROLL_W = 1024
MXU_TILE = 256
FFN_HIDDEN_CHUNKS = ((0, 6 * MXU_TILE), (6 * MXU_TILE, FFN_HIDDEN))
HG_CHUNK = 256
HG_LEVELS = (1, 2, 4, 8, 16, 32, 64, 128)


def _compiler_params(n_grid):
    return pltpu.CompilerParams(
        dimension_semantics=("arbitrary",) * n_grid,
        vmem_limit_bytes=VMEM_LIMIT_BYTES)


def _rms(x, g):
    return x * lax.rsqrt(jnp.mean(x * x, axis=-1, keepdims=True) + EPS) * g


def _gelu(x):
    return 0.5 * x * (1.0 + jnp.tanh(0.7978845608028654 * (x + 0.044715 * x * x * x)))


def _sigmoid(x):
    return 1.0 / (1.0 + jnp.exp(-x))


def _silu(x):
    return x * _sigmoid(x)


def _dot(a, b):
    return jnp.dot(a, b, preferred_element_type=F32)


def _dot_nt(a, b):
    return lax.dot_general(a, b, (((1,), (1,)), ((), ())), preferred_element_type=F32)


def _dot_tn(a, b):
    return lax.dot_general(a, b, (((0,), (0,)), ((), ())), preferred_element_type=F32)


def _split2(x):
    hi = x.astype(BF16)
    lo = (x - hi.astype(F32)).astype(BF16)
    return hi, lo


def _head_of_lane(shape, dim):
    return lax.broadcasted_iota(jnp.int32, shape, dim) // HEAD_DIM


def _layer_spec(shape, layer):
    zeros = (0,) * len(shape)
    return pl.BlockSpec((None,) + tuple(shape), lambda i: (layer,) + zeros,
                        pipeline_mode=pl.Buffered(1))


def _sub_tiles(n_rows):
    half = n_rows // 2
    return (pl.ds(0, half), pl.ds(half, half))


def _in_proj_kernel(x_ref, g_ref, w_ref, za_ref, zr_ref):
    for rows in _sub_tiles(x_ref.shape[0]):
        h = _rms(x_ref[rows, :], g_ref[...]).astype(BF16)
        z = _dot(h, w_ref[...])
        za_ref[rows, 0:MIX_W] = (z[:, 0:MIX_W] * ATT_Q_SCALE).astype(BF16)
        za_ref[rows, MIX_W:N_ATT_COLS] = z[:, MIX_W:N_ATT_COLS].astype(BF16)
        zr_ref[rows, :] = z[:, N_ATT_COLS:]


def _in_proj(x2, layer, g, w_mix):
    t = x2.shape[0]
    n_rest = N_MIX_COLS - N_ATT_COLS
    return pl.pallas_call(
        _in_proj_kernel,
        grid=(t // TOKEN_TILE,),
        in_specs=[
            pl.BlockSpec((TOKEN_TILE, D_MODEL), lambda i: (i, 0)),
            _layer_spec((1, D_MODEL), layer),
            _layer_spec((D_MODEL, N_MIX_COLS), layer),
        ],
        out_specs=[pl.BlockSpec((TOKEN_TILE, N_ATT_COLS), lambda i: (i, 0)),
                   pl.BlockSpec((TOKEN_TILE, n_rest), lambda i: (i, 0))],
        out_shape=[jax.ShapeDtypeStruct((t, N_ATT_COLS), BF16),
                   jax.ShapeDtypeStruct((t, n_rest), F32)],
        compiler_params=_compiler_params(1),
        name="in_proj",
    )(x2, g, w_mix)


def _attn_kernel(q_ref, k_ref, v_ref, brow_ref, o_ref, bias_scr):
    s_len = q_ref.shape[0]

    @pl.when(pl.program_id(0) == 0)
    def _():
        qi = lax.broadcasted_iota(jnp.int32, (ATT_QB, ATT_KB), 0) // CHUNK
        kc = lax.broadcasted_iota(jnp.int32, (ATT_QB, ATT_KB), 1) // CHUNK
        band = (kc >= qi) & (kc <= qi + ATT_LEFT_CHUNKS)
        for h in range(N_HEADS):
            rows = jnp.broadcast_to(brow_ref[h:h + 1, :], (ATT_QB, ATT_ROLL_W))
            tile = pltpu.roll(rows, ATT_ROLL_W - ATT_QB, 1, stride=1, stride_axis=0)
            bias_scr[h] = jnp.where(band, tile[:, :ATT_KB], NEG_BIG)

    lane_head = _head_of_lane((ATT_QB, MIX_W), 1)
    head_rows = [jnp.where(_head_of_lane((1, MIX_W), 1) == h, 1.0, 0.0).astype(BF16)
                 for h in range(N_HEADS)]

    def block(r0, k0, n_keys):
        q = q_ref[pl.ds(r0, ATT_QB), :]
        kb = k_ref[pl.ds(k0, n_keys), :]
        vb = v_ref[pl.ds(k0, n_keys), :]
        acc = jnp.zeros((ATT_QB, MIX_W), F32)
        for h in range(N_HEADS):
            s = _dot_nt(q * head_rows[h], kb) + bias_scr[h, :, pl.ds(ATT_KB - n_keys, n_keys)]
            m = jnp.max(s, axis=-1, keepdims=True)
            p = jnp.exp2(s - m)
            l = jnp.sum(p, axis=-1, keepdims=True)
            o = _dot(p.astype(BF16), vb)
            acc = acc + jnp.where(lane_head == h, o * (1.0 / l), 0.0)
        o_ref[pl.ds(r0, ATT_QB), :] = acc.astype(o_ref.dtype)

    n_head_blocks = ATT_PAD // ATT_QB
    for j in range(n_head_blocks):
        block(j * ATT_QB, 0, (j + 1) * ATT_QB)

    def body(j, carry):
        r0 = pl.multiple_of(j * ATT_QB, ATT_QB)
        block(r0, r0 - ATT_PAD, ATT_KB)
        return carry

    lax.fori_loop(n_head_blocks, s_len // ATT_QB, body, 0, unroll=3)


def _attn_bias_rows(rel_bias):
    dist = ATT_PAD + ATT_QB - np.arange(ATT_ROLL_W)
    idx = np.clip(dist, -(CHUNK - 1), REL_MAX) + (CHUNK - 1)
    return rel_bias.astype(F32)[:, :, idx] * LOG2_E


def _attention(za3, layer, bias_rows):
    b, s, _ = za3.shape
    col_spec = lambda c: pl.BlockSpec((None, s, MIX_W), lambda i, c=c: (i, 0, c))
    return pl.pallas_call(
        _attn_kernel,
        grid=(b,),
        in_specs=[col_spec(0), col_spec(1), col_spec(2),
                  _layer_spec((N_HEADS, ATT_ROLL_W), layer)],
        out_specs=pl.BlockSpec((None, s, MIX_W), lambda i: (i, 0, 0)),
        out_shape=jax.ShapeDtypeStruct((b, s, MIX_W), BF16),
        scratch_shapes=[pltpu.VMEM((N_HEADS, ATT_QB, ATT_KB), F32)],
        compiler_params=_compiler_params(1),
        name="attention",
    )(za3, za3, za3, bias_rows)


def _hgrn_coef_matrix():
    c = HG_CHUNK
    t = np.arange(c)[:, None]
    r = np.arange(c)[None, :]
    mats = [(r <= t)]
    for g in HG_LEVELS:
        bnd = (t // (2 * g)) * (2 * g) + g - 1
        right = (t // g) % 2 == 1
        m_right = (r > bnd) & (r <= t)
        m_left = (r > t) & (r <= bnd)
        mats.append(np.where(right, m_right, m_left))
    return np.concatenate(mats, axis=0).astype(np.float32)


def _hgrn_consts():
    return (jnp.asarray(_hgrn_coef_matrix(), dtype=BF16),)


def _hgrn_kernel(layer, q_ref, f_ref, i_ref, g_ref, lbl_ref, ng_ref, coef_ref, o_ref, st_ref):
    s_len = q_ref.shape[0]
    c = HG_CHUNK
    logits = lbl_ref[...]
    e = jnp.exp(logits - jnp.max(logits, axis=0, keepdims=True))
    p = e / jnp.sum(e, axis=0, keepdims=True)
    lb = jnp.zeros((1, MIX_W), F32)
    for l in range(1, layer + 1):
        lb = lb + p[l:l + 1, :]

    st_ref[...] = jnp.zeros((MIX_W, MIX_W), F32)
    row = lax.broadcasted_iota(jnp.int32, (c, c), 0)
    colm = lax.broadcasted_iota(jnp.int32, (c, c), 1)
    lane_head = _head_of_lane((c, MIX_W), 1)
    same_head = (_head_of_lane((MIX_W, MIX_W), 0) == _head_of_lane((MIX_W, MIX_W), 1))
    head_ones = jnp.where(same_head, 1.0, 0.0).astype(BF16)
    trow = lax.broadcasted_iota(jnp.int32, (c, MIX_W), 0)

    def body(j, carry):
        r0 = pl.multiple_of(j * c, c)
        fz = f_ref[pl.ds(r0, c), :]
        qf = _silu(q_ref[pl.ds(r0, c), :])
        f = lb + (1.0 - lb) * _sigmoid(fz)
        lf = jnp.log(jnp.maximum(f, LOG_FLOOR))
        kf = (1.0 - lb) * _sigmoid(-fz)
        v = i_ref[pl.ds(r0, c), :]
        vb = v.astype(BF16)

        lf_hi, lf_lo = _split2(lf)
        pc = _dot(coef_ref[0:c, :], lf_hi) + _dot(coef_ref[0:c, :], lf_lo)
        sums = _dot(coef_ref[c:, :], lf_hi)
        ptot = pc[c - 1:c, :]

        st = st_ref[...]
        qd = (qf * jnp.exp(pc)).astype(BF16)
        o = _dot_nt(qd, st.astype(BF16))
        kd = (kf * jnp.exp(ptot - pc)).astype(BF16)
        st_ref[...] = st * jnp.exp(ptot) + jnp.where(same_head, _dot_tn(vb, kd), 0.0)

        o = o + _dot((qf * kf).astype(BF16), head_ones) * v

        qts, kts = [], []
        for li, g in enumerate(HG_LEVELS):
            eg = jnp.exp(sums[li * c:(li + 1) * c])
            right = (trow // g) % 2 == 1
            qts.append(jnp.where(right, qf * eg, 0.0))
            kts.append(jnp.where(right, 0.0, kf * eg).astype(BF16))
        for h in range(N_HEADS):
            a = jnp.zeros((c, c), F32)
            for li, g in enumerate(HG_LEVELS):
                qm = jnp.where(lane_head == h, qts[li], 0.0).astype(BF16)
                sc = _dot_nt(qm, kts[li])
                if 2 * g < c:
                    sc = jnp.where((row // (2 * g)) == (colm // (2 * g)), sc, 0.0)
                a = a + sc
            o = o + jnp.where(lane_head == h, _dot(a.astype(BF16), vb), 0.0)

        ms = _dot((o * o).astype(BF16), head_ones) * (1.0 / HEAD_DIM)
        o = o * lax.rsqrt(ms + EPS) * ng_ref[...]
        o = o * _silu(g_ref[pl.ds(r0, c), :])
        o_ref[pl.ds(r0, c), :] = o.astype(o_ref.dtype)
        return carry

    lax.fori_loop(0, s_len // c, body, 0, unroll=4)


def _hgrn2(z3, layer, lb_logits, norm_g, coef):
    b, s, _ = z3.shape
    col_spec = lambda c: pl.BlockSpec((None, s, MIX_W), lambda i, c=c: (i, 0, c))
    return pl.pallas_call(
        functools.partial(_hgrn_kernel, layer),
        grid=(b,),
        in_specs=[col_spec(0), col_spec(1), col_spec(2), col_spec(3),
                  pl.BlockSpec((DEPTH, MIX_W), lambda i: (0, 0)),
                  _layer_spec((1, MIX_W), layer),
                  pl.BlockSpec(coef.shape, lambda i: (0, 0))],
        out_specs=pl.BlockSpec((None, s, MIX_W), lambda i: (i, 0, 0)),
        out_shape=jax.ShapeDtypeStruct((b, s, MIX_W), BF16),
        scratch_shapes=[pltpu.VMEM((MIX_W, MIX_W), F32)],
        compiler_params=_compiler_params(1),
        name="hgrn2",
    )(z3, z3, z3, z3, lb_logits, norm_g, coef)


def _gmlp_kernel(u_ref, v_ref, ng_ref, ws_ref, bias_ref, o_ref):
    s_len = u_ref.shape[0]
    prow = lax.broadcasted_iota(jnp.int32, (GM_BLOCK, GM_BLOCK), 0)
    pcol = lax.broadcasted_iota(jnp.int32, (GM_BLOCK, GM_BLOCK), 1)
    lane_head = _head_of_lane((GM_BLOCK, MIX_W), 1)
    ws = [jnp.where(pcol <= prow, ws_ref[g], 0.0).astype(BF16) for g in range(N_HEADS)]

    def body(j, carry):
        r0 = pl.multiple_of(j * GM_BLOCK, GM_BLOCK)
        vn = _rms(_gelu(v_ref[pl.ds(r0, GM_BLOCK), :]), ng_ref[...]).astype(BF16)
        mixed = bias_ref[...]
        for g in range(N_HEADS):
            mixed = mixed + jnp.where(lane_head == g, _dot(ws[g], vn), 0.0)
        u = _gelu(u_ref[pl.ds(r0, GM_BLOCK), :])
        o_ref[pl.ds(r0, GM_BLOCK), :] = (u * mixed).astype(o_ref.dtype)
        return carry

    lax.fori_loop(0, s_len // GM_BLOCK, body, 0, unroll=2)


def _gmlp(z3, layer, norm_g, ws, bias):
    b, s, _ = z3.shape
    col_spec = lambda c: pl.BlockSpec((None, s, MIX_W), lambda i, c=c: (i, 0, c))
    return pl.pallas_call(
        _gmlp_kernel,
        grid=(b,),
        in_specs=[col_spec(4), col_spec(5),
                  _layer_spec((1, MIX_W), layer),
                  _layer_spec((N_HEADS, GM_BLOCK, GM_BLOCK), layer),
                  _layer_spec((GM_BLOCK, MIX_W), layer)],
        out_specs=pl.BlockSpec((None, s, MIX_W), lambda i: (i, 0, 0)),
        out_shape=jax.ShapeDtypeStruct((b, s, MIX_W), BF16),
        compiler_params=_compiler_params(1),
        name="gmlp",
    )(z3, z3, norm_g, ws, bias)


LRU_ROW_CONV_B, LRU_ROW_BA, LRU_ROW_BX, LRU_ROW_LAM = 4, 5, 6, 7
LRU_BLOCK = 64


def _lru_kernel(x_ref, gate_ref, vec_ref, w_ref, o_ref, a_ref, b_ref):
    s_len = x_ref.shape[0]
    vrow = lambda k: vec_ref[k:k + 1, :]
    x = x_ref[...]
    t = lax.broadcasted_iota(jnp.int32, (s_len, MIX_W), 0)
    xc = vrow(LRU_ROW_CONV_B) + x * vrow(CONV_W - 1)
    for d in range(1, CONV_W):
        xs = jnp.where(t >= d, pltpu.roll(x, d, 0), 0.0)
        xc = xc + xs * vrow(CONV_W - 1 - d)
    xcb = xc.astype(BF16)
    r = _sigmoid(_dot(xcb, w_ref[0]) + vrow(LRU_ROW_BA))
    ig = _sigmoid(_dot(xcb, w_ref[1]) + vrow(LRU_ROW_BX))
    nl = -vrow(LRU_ROW_LAM)
    softplus = jnp.maximum(nl, 0.0) + jnp.log(1.0 + jnp.exp(-jnp.abs(nl)))
    log_a = -LRU_C * r * softplus
    mult = jnp.sqrt(jnp.maximum(1.0 - jnp.exp(2.0 * log_a), 0.0))
    mult = jnp.where(t == 0, 1.0, mult)
    a_ref[...] = jnp.exp(log_a)
    b_ref[...] = mult * (ig * xc)
    t_in_block = t % LRU_BLOCK
    d = 1
    while d < LRU_BLOCK:
        a = a_ref[...]
        b = b_ref[...]
        ok = t_in_block >= d
        a_prev = jnp.where(ok, pltpu.roll(a, d, 0), 1.0)
        b_prev = jnp.where(ok, pltpu.roll(b, d, 0), 0.0)
        b_ref[...] = a * b_prev + b
        a_ref[...] = a * a_prev
        d *= 2
    carry = jnp.zeros((1, MIX_W), F32)
    for r0 in range(0, s_len, LRU_BLOCK):
        rows = pl.ds(r0, LRU_BLOCK)
        h = b_ref[rows, :] + a_ref[rows, :] * carry
        o_ref[rows, :] = (h * _gelu(gate_ref[rows, :])).astype(o_ref.dtype)
        carry = h[LRU_BLOCK - 1:LRU_BLOCK, :]


def _block_diag(w):
    eye = jnp.eye(N_HEADS, dtype=w.dtype)
    bd = jnp.einsum('...hij,hg->...higj', w, eye)
    return bd.reshape(w.shape[:-3] + (MIX_W, MIX_W))


def _rg_lru(z3, layer, vec, w_bd):
    b, s, _ = z3.shape
    col_spec = lambda c: pl.BlockSpec((None, s, MIX_W), lambda i, c=c: (i, 0, c))
    return pl.pallas_call(
        _lru_kernel,
        grid=(b,),
        in_specs=[col_spec(6), col_spec(7),
                  _layer_spec((8, MIX_W), layer),
                  _layer_spec((2, MIX_W, MIX_W), layer)],
        out_specs=pl.BlockSpec((None, s, MIX_W), lambda i: (i, 0, 0)),
        out_shape=jax.ShapeDtypeStruct((b, s, MIX_W), BF16),
        scratch_shapes=[pltpu.VMEM((s, MIX_W), F32), pltpu.VMEM((s, MIX_W), F32)],
        compiler_params=_compiler_params(1),
        name="rg_lru",
    )(z3, z3, vec, w_bd)


def _mix_out_kernel(x_ref, oa_ref, ob_ref, oc_ref, od_ref, gpre_ref, gpost_ref,
                    wg_ref, wb_ref, wo_ref, out_ref):
    for rows in _sub_tiles(x_ref.shape[0]):
        x = x_ref[rows, :]
        h = _rms(x, gpre_ref[...]).astype(BF16)
        merged = jnp.zeros(x.shape, F32)
        for n, o_ref in enumerate((oa_ref, ob_ref, oc_ref, od_ref)):
            gate = _dot(h, wg_ref[:, n * D_MODEL:(n + 1) * D_MODEL])
            proj = _dot(o_ref[rows, :], wb_ref[n])
            merged = merged + _sigmoid(gate) * proj
        y = _dot(merged.astype(BF16), wo_ref[...])
        out_ref[rows, :] = x + _rms(y, gpost_ref[...])


def _mix_out(x2, outs, layer, g_pre, g_post, w_gate, w_branch, w_out):
    t = x2.shape[0]
    tok = lambda w: pl.BlockSpec((TOKEN_TILE, w), lambda i: (i, 0))
    return pl.pallas_call(
        _mix_out_kernel,
        grid=(t // TOKEN_TILE,),
        in_specs=[tok(D_MODEL), tok(MIX_W), tok(MIX_W), tok(MIX_W), tok(MIX_W),
                  _layer_spec((1, D_MODEL), layer), _layer_spec((1, D_MODEL), layer),
                  _layer_spec((D_MODEL, N_BRANCH * D_MODEL), layer),
                  _layer_spec((N_BRANCH, MIX_W, D_MODEL), layer),
                  _layer_spec((D_MODEL, D_MODEL), layer)],
        out_specs=tok(D_MODEL),
        out_shape=jax.ShapeDtypeStruct((t, D_MODEL), F32),
        compiler_params=_compiler_params(1),
        name="mix_out",
    )(x2, *outs, g_pre, g_post, w_gate, w_branch, w_out)


def _ffn_kernel(x_ref, gpre_ref, gpost_ref, w1_ref, w2_ref, out_ref):
    for rows in _sub_tiles(x_ref.shape[0]):
        x = x_ref[rows, :]
        h = _rms(x, gpre_ref[...]).astype(BF16)
        y = jnp.zeros(x.shape, F32)
        for lo, hi in FFN_HIDDEN_CHUNKS:
            gt = _dot(h, w1_ref[:, lo:hi])
            up = _dot(h, w1_ref[:, FFN_HIDDEN + lo:FFN_HIDDEN + hi])
            act = (_silu(gt) * up).astype(BF16)
            y = y + _dot(act, w2_ref[lo:hi, :])
        out_ref[rows, :] = x + _rms(y, gpost_ref[...])


def _ffn(x2, layer, g_pre, g_post, w1, w2):
    t = x2.shape[0]
    tile = TOKEN_TILE
    tok = pl.BlockSpec((tile, D_MODEL), lambda i: (i, 0))
    return pl.pallas_call(
        _ffn_kernel,
        grid=(t // tile,),
        in_specs=[tok, _layer_spec((1, D_MODEL), layer), _layer_spec((1, D_MODEL), layer),
                  _layer_spec((D_MODEL, 2 * FFN_HIDDEN), layer),
                  _layer_spec((FFN_HIDDEN, D_MODEL), layer)],
        out_specs=tok,
        out_shape=jax.ShapeDtypeStruct((t, D_MODEL), F32),
        compiler_params=_compiler_params(1),
        name="ffn",
    )(x2, g_pre, g_post, w1, w2)


def kernel(x, norm_mix_pre, norm_mix_post, norm_ffn_pre, norm_ffn_post, w_in, attn_rel_bias, hgrn_lb_logits, hgrn_norm_g, gmlp_norm_g, gmlp_ws, gmlp_bs, lru_conv_w, lru_conv_b, lru_wa, lru_ba, lru_wx, lru_bx, lru_lambda, w_branch, w_out, w_ffn_in, w_ffn_out):
    b, s, d = x.shape
    x2 = x.reshape(b * s, d)
    w_mix_b = w_in[:, :, :N_MIX_COLS].astype(BF16)
    w_gate_b = w_in[:, :, N_MIX_COLS:].astype(BF16)
    w_branch_b = w_branch.astype(BF16)
    w_out_b = w_out.astype(BF16)
    w1_b = w_ffn_in.astype(BF16)
    w2_b = w_ffn_out.astype(BF16)
    row = lambda p: p.reshape(DEPTH, 1, p.shape[-1])
    g_mix_pre, g_mix_post = row(norm_mix_pre), row(norm_mix_post)
    g_ffn_pre, g_ffn_post = row(norm_ffn_pre), row(norm_ffn_post)
    bias_rows = _attn_bias_rows(attn_rel_bias)
    hg_consts = _hgrn_consts()
    hg_norm = row(hgrn_norm_g)
    gm_norm = row(gmlp_norm_g)
    gm_bias = jnp.repeat(jnp.swapaxes(gmlp_bs, 1, 2), HEAD_DIM, axis=2)
    lru_vec = jnp.concatenate(
        [lru_conv_w, lru_conv_b[:, None], lru_ba[:, None], lru_bx[:, None], lru_lambda[:, None]],
        axis=1)
    lru_w = _block_diag(jnp.stack([lru_wa, lru_wx], axis=1)).astype(BF16)
    for l in range(DEPTH):
        z_att, z_rest = _in_proj(x2, l, g_mix_pre, w_mix_b)
        z3 = z_rest.reshape(b, s, N_MIX_COLS - N_ATT_COLS)
        o_a = _attention(z_att.reshape(b, s, N_ATT_COLS), l, bias_rows)
        o_b = _hgrn2(z3, l, hgrn_lb_logits, hg_norm, *hg_consts)
        o_c = _gmlp(z3, l, gm_norm, gmlp_ws, gm_bias)
        o_d = _rg_lru(z3, l, lru_vec, lru_w)
        outs = [o.reshape(b * s, MIX_W) for o in (o_a, o_b, o_c, o_d)]
        x2 = _mix_out(x2, outs, l, g_mix_pre, g_mix_post, w_gate_b, w_branch_b, w_out_b)
        x2 = _ffn(x2, l, g_ffn_pre, g_ffn_post, w1_b, w2_b)
    return x2.reshape(b, s, d)
```

```python
import functools

import numpy as np
import jax
import jax.numpy as jnp
from jax import lax
from jax.experimental import pallas as pl
from jax.experimental.pallas import tpu as pltpu

D_MODEL = 1024
DEPTH = 4
CHUNK = 64
EPS = 1e-6
N_HEADS = 4
HEAD_DIM = 64
MIX_W = N_HEADS * HEAD_DIM
N_BRANCH = 4
ATT_LEFT_CHUNKS = 8
REL_MAX = 256
NEG_BIG = -1e30
LOG_FLOOR = 1e-30
GM_BLOCK = 128
CONV_W = 4
LRU_C = 8.0
FFN_HIDDEN = 2816
N_MIX_COLS = 11 * MIX_W
N_ATT_COLS = 3 * MIX_W
LOG2_E = 1.4426950408889634
ATT_Q_SCALE = HEAD_DIM ** -0.5 * LOG2_E

F32 = jnp.float32
BF16 = jnp.bfloat16

VMEM_LIMIT_BYTES = 56 * 1024 * 1024
TOKEN_TILE = 1024
FFN_TOKEN_TILE = 512
SUB_TILE = 256
ATT_QB = 256
ATT_PAD = ATT_LEFT_CHUNKS * CHUNK
ATT_KB = ATT_PAD + ATT_QB
ATT_ROLL_W = 1024
MXU_TILE = 256
FFN_HIDDEN_CHUNKS = ((0, 6 * MXU_TILE), (6 * MXU_TILE, FFN_HIDDEN))
HG_CHUNK = 256
SUBLANES = 8
HG_BASE = 4
HG_LEVELS = (4, 8, 16, 32, 64, 128)


def _compiler_params(n_grid):
    return pltpu.CompilerParams(
        dimension_semantics=("arbitrary",) * n_grid,
        vmem_limit_bytes=VMEM_LIMIT_BYTES)


def _rms(x, g):
    return x * lax.rsqrt(jnp.mean(x * x, axis=-1, keepdims=True) + EPS) * g


def _gelu(x):
    return 0.5 * x * (1.0 + jnp.tanh(0.7978845608028654 * (x + 0.044715 * x * x * x)))


def _sigmoid(x):
    return 1.0 / (1.0 + jnp.exp(-x))


def _silu(x):
    return x * _sigmoid(x)


def _dot(a, b):
    return jnp.dot(a, b, preferred_element_type=F32)


def _dot_nt(a, b):
    return lax.dot_general(a, b, (((1,), (1,)), ((), ())), preferred_element_type=F32)


def _dot_tn(a, b):
    return lax.dot_general(a, b, (((0,), (0,)), ((), ())), preferred_element_type=F32)


def _split2(x):
    hi = x.astype(BF16)
    lo = (x - hi.astype(F32)).astype(BF16)
    return hi, lo


def _head_of_lane(shape, dim):
    return lax.broadcasted_iota(jnp.int32, shape, dim) // HEAD_DIM


def _layer_spec(shape, layer):
    zeros = (0,) * len(shape)
    return pl.BlockSpec((None,) + tuple(shape), lambda i: (layer,) + zeros,
                        pipeline_mode=pl.Buffered(1))


def _split_cast_kernel(w_ref, mix_ref, gate_ref):
    mix_ref[...] = w_ref[:, 0:N_MIX_COLS].astype(BF16)
    gate_ref[...] = w_ref[:, N_MIX_COLS:].astype(BF16)


def _split_cast_w_in(w_in):
    depth, d, n_cols = w_in.shape
    n_gate = n_cols - N_MIX_COLS
    rows = MXU_TILE
    blk = lambda n: pl.BlockSpec((None, rows, n), lambda l, i: (l, i, 0))
    return pl.pallas_call(
        _split_cast_kernel,
        grid=(depth, d // rows),
        in_specs=[blk(n_cols)],
        out_specs=[blk(N_MIX_COLS), blk(n_gate)],
        out_shape=[jax.ShapeDtypeStruct((depth, d, N_MIX_COLS), BF16),
                   jax.ShapeDtypeStruct((depth, d, n_gate), BF16)],
        compiler_params=_compiler_params(2),
        name="split_cast_w_in",
    )(w_in)


def _sub_tiles(n_rows):
    return tuple(pl.ds(r, SUB_TILE) for r in range(0, n_rows, SUB_TILE))


def _in_proj_kernel(x_ref, g_ref, w_ref, za_ref, zr_ref):
    for rows in _sub_tiles(x_ref.shape[0]):
        h = _rms(x_ref[rows, :], g_ref[...]).astype(BF16)
        z = _dot(h, w_ref[...])
        za_ref[rows, 0:MIX_W] = (z[:, 0:MIX_W] * ATT_Q_SCALE).astype(BF16)
        za_ref[rows, MIX_W:N_ATT_COLS] = z[:, MIX_W:N_ATT_COLS].astype(BF16)
        zr_ref[rows, :] = z[:, N_ATT_COLS:]


def _in_proj(x2, layer, g, w_mix):
    t = x2.shape[0]
    n_rest = N_MIX_COLS - N_ATT_COLS
    return pl.pallas_call(
        _in_proj_kernel,
        grid=(t // TOKEN_TILE,),
        in_specs=[
            pl.BlockSpec((TOKEN_TILE, D_MODEL), lambda i: (i, 0)),
            _layer_spec((1, D_MODEL), layer),
            _layer_spec((D_MODEL, N_MIX_COLS), layer),
        ],
        out_specs=[pl.BlockSpec((TOKEN_TILE, N_ATT_COLS), lambda i: (i, 0)),
                   pl.BlockSpec((TOKEN_TILE, n_rest), lambda i: (i, 0))],
        out_shape=[jax.ShapeDtypeStruct((t, N_ATT_COLS), BF16),
                   jax.ShapeDtypeStruct((t, n_rest), F32)],
        compiler_params=_compiler_params(1),
        name="in_proj",
    )(x2, g, w_mix)


def _attn_kernel(q_ref, k_ref, v_ref, brow_ref, o_ref, bias_scr):
    s_len = q_ref.shape[0]

    @pl.when(pl.program_id(0) == 0)
    def _():
        qi = lax.broadcasted_iota(jnp.int32, (ATT_QB, ATT_KB), 0) // CHUNK
        kc = lax.broadcasted_iota(jnp.int32, (ATT_QB, ATT_KB), 1) // CHUNK
        band = (kc >= qi) & (kc <= qi + ATT_LEFT_CHUNKS)
        for h in range(N_HEADS):
            rows = jnp.broadcast_to(brow_ref[h:h + 1, :], (ATT_QB, ATT_ROLL_W))
            tile = pltpu.roll(rows, ATT_ROLL_W - ATT_QB, 1, stride=1, stride_axis=0)
            bias_scr[h] = jnp.where(band, tile[:, :ATT_KB], NEG_BIG)

    lane_head = _head_of_lane((ATT_QB, MIX_W), 1)
    head_rows = [jnp.where(_head_of_lane((1, MIX_W), 1) == h, 1.0, 0.0).astype(BF16)
                 for h in range(N_HEADS)]

    def block(r0, k0, n_keys):
        q = q_ref[pl.ds(r0, ATT_QB), :]
        kb = k_ref[pl.ds(k0, n_keys), :]
        vb = v_ref[pl.ds(k0, n_keys), :]
        acc = jnp.zeros((ATT_QB, MIX_W), F32)
        for h in range(N_HEADS):
            s = _dot_nt(q * head_rows[h], kb) + bias_scr[h, :, pl.ds(ATT_KB - n_keys, n_keys)]
            m = jnp.max(s, axis=-1, keepdims=True)
            p = jnp.exp2(s - m)
            l = jnp.sum(p, axis=-1, keepdims=True)
            o = _dot(p.astype(BF16), vb)
            acc = acc + jnp.where(lane_head == h, o * (1.0 / l), 0.0)
        o_ref[pl.ds(r0, ATT_QB), :] = acc.astype(o_ref.dtype)

    n_head_blocks = ATT_PAD // ATT_QB
    for j in range(n_head_blocks):
        block(j * ATT_QB, 0, (j + 1) * ATT_QB)

    def body(j, carry):
        r0 = pl.multiple_of(j * ATT_QB, ATT_QB)
        block(r0, r0 - ATT_PAD, ATT_KB)
        return carry

    lax.fori_loop(n_head_blocks, s_len // ATT_QB, body, 0, unroll=3)


def _attn_bias_rows(rel_bias):
    dist = ATT_PAD + ATT_QB - np.arange(ATT_ROLL_W)
    idx = np.clip(dist, -(CHUNK - 1), REL_MAX) + (CHUNK - 1)
    return rel_bias.astype(F32)[:, :, idx] * LOG2_E


def _attention(za3, layer, bias_rows):
    b, s, _ = za3.shape
    col_spec = lambda c: pl.BlockSpec((None, s, MIX_W), lambda i, c=c: (i, 0, c))
    return pl.pallas_call(
        _attn_kernel,
        grid=(b,),
        in_specs=[col_spec(0), col_spec(1), col_spec(2),
                  _layer_spec((N_HEADS, ATT_ROLL_W), layer)],
        out_specs=pl.BlockSpec((None, s, MIX_W), lambda i: (i, 0, 0)),
        out_shape=jax.ShapeDtypeStruct((b, s, MIX_W), BF16),
        scratch_shapes=[pltpu.VMEM((N_HEADS, ATT_QB, ATT_KB), F32)],
        compiler_params=_compiler_params(1),
        name="attention",
    )(za3, za3, za3, bias_rows)


def _hgrn_half_rows(g, right):
    off = g if right else 0
    return np.concatenate([np.arange(2 * g * b + off, 2 * g * b + off + g)
                           for b in range(HG_CHUNK // (2 * g))])


def _hgrn_coef_matrix():
    r = np.arange(HG_CHUNK)[None, :]
    mats = [r <= np.arange(HG_CHUNK)[:, None]]
    for g in HG_LEVELS:
        for right in (True, False):
            t = _hgrn_half_rows(g, right)[:, None]
            bnd = (t // (2 * g)) * (2 * g) + g - 1
            mats.append((r > bnd) & (r <= t) if right else (r > t) & (r <= bnd))
    return np.concatenate(mats, axis=0).astype(np.float32)


def _hgrn_level_masks():
    half = HG_CHUNK // 2
    tc = np.arange(half)[:, None]
    sc = np.arange(2 * half)[None, :] % half
    return np.stack([(tc // g) == (sc // g) for g in HG_LEVELS]).astype(np.float32)


def _hgrn_consts():
    return (jnp.asarray(_hgrn_coef_matrix(), dtype=BF16), jnp.asarray(_hgrn_level_masks()))


def _hgrn_kernel(layer, q_ref, f_ref, i_ref, g_ref, lbl_ref, ng_ref, coef_ref, lmask_ref, o_ref):
    s_len = q_ref.shape[0]
    c = HG_CHUNK
    half = c // 2
    logits = lbl_ref[...]
    e = jnp.exp(logits - jnp.max(logits, axis=0, keepdims=True))
    p = e / jnp.sum(e, axis=0, keepdims=True)
    lb = jnp.zeros((1, MIX_W), F32)
    for l in range(1, layer + 1):
        lb = lb + p[l:l + 1, :]

    same_head = (_head_of_lane((MIX_W, MIX_W), 0) == _head_of_lane((MIX_W, MIX_W), 1))
    head_ones = jnp.where(same_head, 1.0, 0.0).astype(BF16)
    head_rows = [jnp.where(_head_of_lane((1, MIX_W), 1) == h, 1.0, 0.0).astype(BF16)
                 for h in range(N_HEADS)]
    group_shape = (c // SUBLANES, SUBLANES, MIX_W)
    row_in_group = lax.broadcasted_iota(jnp.int32, group_shape, 1) % HG_BASE

    def body(j, st):
        r0 = pl.multiple_of(j * c, c)
        fz = f_ref[pl.ds(r0, c), :]
        qf = _silu(q_ref[pl.ds(r0, c), :])
        sg = _sigmoid(fz)
        f = lb + (1.0 - lb) * sg
        lf = jnp.log2(jnp.maximum(f, LOG_FLOOR))
        kf = (1.0 - lb) * (1.0 - sg)
        v = i_ref[pl.ds(r0, c), :]
        vb = v.astype(BF16)

        lf_hi, lf_lo = _split2(lf)
        pc = _dot(coef_ref[0:c, :], lf_hi) + _dot(coef_ref[0:c, :], lf_lo)
        sums = _dot(coef_ref[c:, :], lf_hi)
        ptot = pc[c - 1:c, :]

        qd = (qf * jnp.exp2(pc)).astype(BF16)
        terms = [_dot_nt(qd, st.astype(BF16))]
        kd = (kf * jnp.exp2(ptot - pc)).astype(BF16)
        st_next = st * jnp.exp2(ptot) + jnp.where(same_head, _dot_tn(vb, kd), 0.0)

        p3 = pc.reshape(group_shape)
        q3, k3, v3 = (x.reshape(group_shape) for x in (qf, kf, v))

        def group_term(d):
            if d == 0:
                prod, vd = q3 * k3, v3
            else:
                e = jnp.exp2(jnp.minimum(p3 - pltpu.roll(p3, d, 1), 0.0))
                prod = jnp.where(row_in_group >= d, q3 * pltpu.roll(k3, d, 1) * e, 0.0)
                vd = pltpu.roll(v3, d, 1)
            w = _dot(prod.reshape(c, MIX_W).astype(BF16), head_ones)
            return w * vd.reshape(c, MIX_W)

        def level_term(li, g):
            n_blocks = half // g

            def half_rows(x, off):
                return jnp.concatenate(
                    [x[2 * g * b + off:2 * g * b + off + g] for b in range(n_blocks)], axis=0)

            cq = sums[li * c:li * c + half]
            ck = sums[li * c + half:(li + 1) * c]
            qt = (half_rows(qf, g) * jnp.exp2(cq)).astype(BF16)
            kt = (half_rows(kf, 0) * jnp.exp2(ck)).astype(BF16)
            vt = half_rows(v, 0).astype(BF16)
            out = jnp.zeros((half, MIX_W), F32)
            for h in range(0, N_HEADS, 2):
                kw = jnp.concatenate([kt * head_rows[h], kt * head_rows[h + 1]], axis=0)
                vw = jnp.concatenate([vt * head_rows[h], vt * head_rows[h + 1]], axis=0)
                sc = _dot_nt(qt, kw) * lmask_ref[li]
                out = out + _dot(sc.astype(BF16), vw)
            zero = jnp.zeros((g, MIX_W), F32)
            return jnp.concatenate(
                [piece for b in range(n_blocks) for piece in (zero, out[b * g:(b + 1) * g])], axis=0)

        for i in range(max(HG_BASE, len(HG_LEVELS))):
            if i < HG_BASE:
                terms.append(group_term(i))
            if i < len(HG_LEVELS):
                terms.append(level_term(i, HG_LEVELS[i]))
        while len(terms) > 1:
            terms = [a + b for a, b in zip(terms[0::2], terms[1::2])] + terms[len(terms) & ~1:]
        o = terms[0]

        ms = _dot((o * o).astype(BF16), head_ones) * (1.0 / HEAD_DIM)
        o = o * lax.rsqrt(ms + EPS) * ng_ref[...]
        o = o * _silu(g_ref[pl.ds(r0, c), :])
        o_ref[pl.ds(r0, c), :] = o.astype(o_ref.dtype)
        return st_next

    lax.fori_loop(0, s_len // c, body, jnp.zeros((MIX_W, MIX_W), F32), unroll=4)


def _hgrn2(z3, layer, lb_logits, norm_g, coef, level_masks):
    b, s, _ = z3.shape
    col_spec = lambda c: pl.BlockSpec((None, s, MIX_W), lambda i, c=c: (i, 0, c))
    return pl.pallas_call(
        functools.partial(_hgrn_kernel, layer),
        grid=(b,),
        in_specs=[col_spec(0), col_spec(1), col_spec(2), col_spec(3),
                  pl.BlockSpec((DEPTH, MIX_W), lambda i: (0, 0)),
                  _layer_spec((1, MIX_W), layer),
                  pl.BlockSpec(coef.shape, lambda i: (0, 0)),
                  pl.BlockSpec(level_masks.shape, lambda i: (0, 0, 0))],
        out_specs=pl.BlockSpec((None, s, MIX_W), lambda i: (i, 0, 0)),
        out_shape=jax.ShapeDtypeStruct((b, s, MIX_W), BF16),
        compiler_params=_compiler_params(1),
        name="hgrn2",
    )(z3, z3, z3, z3, lb_logits, norm_g, coef, level_masks)


def _gmlp_kernel(u_ref, v_ref, ng_ref, ws_ref, bias_ref, o_ref):
    s_len = u_ref.shape[0]
    prow = lax.broadcasted_iota(jnp.int32, (GM_BLOCK, GM_BLOCK), 0)
    pcol = lax.broadcasted_iota(jnp.int32, (GM_BLOCK, GM_BLOCK), 1)
    lane_head = _head_of_lane((GM_BLOCK, MIX_W), 1)
    ws = [jnp.where(pcol <= prow, ws_ref[g], 0.0).astype(BF16) for g in range(N_HEADS)]

    def body(j, carry):
        r0 = pl.multiple_of(j * GM_BLOCK, GM_BLOCK)
        vn = _rms(_gelu(v_ref[pl.ds(r0, GM_BLOCK), :]), ng_ref[...]).astype(BF16)
        mixed = bias_ref[...]
        for g in range(N_HEADS):
            mixed = mixed + jnp.where(lane_head == g, _dot(ws[g], vn), 0.0)
        u = _gelu(u_ref[pl.ds(r0, GM_BLOCK), :])
        o_ref[pl.ds(r0, GM_BLOCK), :] = (u * mixed).astype(o_ref.dtype)
        return carry

    lax.fori_loop(0, s_len // GM_BLOCK, body, 0, unroll=2)


def _gmlp(z3, layer, norm_g, ws, bias):
    b, s, _ = z3.shape
    col_spec = lambda c: pl.BlockSpec((None, s, MIX_W), lambda i, c=c: (i, 0, c))
    return pl.pallas_call(
        _gmlp_kernel,
        grid=(b,),
        in_specs=[col_spec(4), col_spec(5),
                  _layer_spec((1, MIX_W), layer),
                  _layer_spec((N_HEADS, GM_BLOCK, GM_BLOCK), layer),
                  _layer_spec((GM_BLOCK, MIX_W), layer)],
        out_specs=pl.BlockSpec((None, s, MIX_W), lambda i: (i, 0, 0)),
        out_shape=jax.ShapeDtypeStruct((b, s, MIX_W), BF16),
        compiler_params=_compiler_params(1),
        name="gmlp",
    )(z3, z3, norm_g, ws, bias)


LRU_ROW_CONV_B, LRU_ROW_BA, LRU_ROW_BX, LRU_ROW_LAM = 4, 5, 6, 7
LRU_BLOCK = 64


def _lru_kernel(x_ref, gate_ref, vec_ref, w_ref, o_ref, a_ref, b_ref):
    s_len = x_ref.shape[0]
    vrow = lambda k: vec_ref[k:k + 1, :]
    x = x_ref[...]
    t = lax.broadcasted_iota(jnp.int32, (s_len, MIX_W), 0)
    xc = vrow(LRU_ROW_CONV_B) + x * vrow(CONV_W - 1)
    for d in range(1, CONV_W):
        xs = jnp.where(t >= d, pltpu.roll(x, d, 0), 0.0)
        xc = xc + xs * vrow(CONV_W - 1 - d)
    xcb = xc.astype(BF16)
    r = _sigmoid(_dot(xcb, w_ref[0]) + vrow(LRU_ROW_BA))
    ig = _sigmoid(_dot(xcb, w_ref[1]) + vrow(LRU_ROW_BX))
    nl = -vrow(LRU_ROW_LAM)
    softplus = jnp.maximum(nl, 0.0) + jnp.log(1.0 + jnp.exp(-jnp.abs(nl)))
    log_a = -LRU_C * r * softplus
    mult = jnp.sqrt(jnp.maximum(1.0 - jnp.exp(2.0 * log_a), 0.0))
    mult = jnp.where(t == 0, 1.0, mult)
    a_ref[...] = jnp.exp(log_a)
    b_ref[...] = mult * (ig * xc)
    t_in_block = t % LRU_BLOCK
    d = 1
    while d < LRU_BLOCK:
        a = a_ref[...]
        b = b_ref[...]
        ok = t_in_block >= d
        a_prev = jnp.where(ok, pltpu.roll(a, d, 0), 1.0)
        b_prev = jnp.where(ok, pltpu.roll(b, d, 0), 0.0)
        b_ref[...] = a * b_prev + b
        a_ref[...] = a * a_prev
        d *= 2
    carry = jnp.zeros((1, MIX_W), F32)
    for r0 in range(0, s_len, LRU_BLOCK):
        rows = pl.ds(r0, LRU_BLOCK)
        h = b_ref[rows, :] + a_ref[rows, :] * carry
        o_ref[rows, :] = (h * _gelu(gate_ref[rows, :])).astype(o_ref.dtype)
        carry = h[LRU_BLOCK - 1:LRU_BLOCK, :]


def _block_diag(w):
    eye = jnp.eye(N_HEADS, dtype=w.dtype)
    bd = jnp.einsum('...hij,hg->...higj', w, eye)
    return bd.reshape(w.shape[:-3] + (MIX_W, MIX_W))


def _rg_lru(z3, layer, vec, w_bd):
    b, s, _ = z3.shape
    col_spec = lambda c: pl.BlockSpec((None, s, MIX_W), lambda i, c=c: (i, 0, c))
    return pl.pallas_call(
        _lru_kernel,
        grid=(b,),
        in_specs=[col_spec(6), col_spec(7),
                  _layer_spec((8, MIX_W), layer),
                  _layer_spec((2, MIX_W, MIX_W), layer)],
        out_specs=pl.BlockSpec((None, s, MIX_W), lambda i: (i, 0, 0)),
        out_shape=jax.ShapeDtypeStruct((b, s, MIX_W), BF16),
        scratch_shapes=[pltpu.VMEM((s, MIX_W), F32), pltpu.VMEM((s, MIX_W), F32)],
        compiler_params=_compiler_params(1),
        name="rg_lru",
    )(z3, z3, vec, w_bd)


def _mix_out_kernel(x_ref, oa_ref, ob_ref, oc_ref, od_ref, gpre_ref, gpost_ref,
                    wg_ref, wb_ref, wo_ref, out_ref):
    for rows in _sub_tiles(x_ref.shape[0]):
        x = x_ref[rows, :]
        h = _rms(x, gpre_ref[...]).astype(BF16)
        merged = jnp.zeros(x.shape, F32)
        for n, o_ref in enumerate((oa_ref, ob_ref, oc_ref, od_ref)):
            gate = _dot(h, wg_ref[:, n * D_MODEL:(n + 1) * D_MODEL])
            proj = _dot(o_ref[rows, :], wb_ref[n])
            merged = merged + _sigmoid(gate) * proj
        y = _dot(merged.astype(BF16), wo_ref[...])
        out_ref[rows, :] = x + _rms(y, gpost_ref[...])


def _mix_out(x2, outs, layer, g_pre, g_post, w_gate, w_branch, w_out):
    t = x2.shape[0]
    tok = lambda w: pl.BlockSpec((TOKEN_TILE, w), lambda i: (i, 0))
    return pl.pallas_call(
        _mix_out_kernel,
        grid=(t // TOKEN_TILE,),
        in_specs=[tok(D_MODEL), tok(MIX_W), tok(MIX_W), tok(MIX_W), tok(MIX_W),
                  _layer_spec((1, D_MODEL), layer), _layer_spec((1, D_MODEL), layer),
                  _layer_spec((D_MODEL, N_BRANCH * D_MODEL), layer),
                  _layer_spec((N_BRANCH, MIX_W, D_MODEL), layer),
                  _layer_spec((D_MODEL, D_MODEL), layer)],
        out_specs=tok(D_MODEL),
        out_shape=jax.ShapeDtypeStruct((t, D_MODEL), F32),
        compiler_params=_compiler_params(1),
        name="mix_out",
    )(x2, *outs, g_pre, g_post, w_gate, w_branch, w_out)


def _ffn_kernel(x_ref, gpre_ref, gpost_ref, w1_ref, w2_ref, out_ref):
    for rows in _sub_tiles(x_ref.shape[0]):
        x = x_ref[rows, :]
        h = _rms(x, gpre_ref[...]).astype(BF16)
        y = jnp.zeros(x.shape, F32)
        for lo, hi in FFN_HIDDEN_CHUNKS:
            gt = _dot(h, w1_ref[:, lo:hi])
            up = _dot(h, w1_ref[:, FFN_HIDDEN + lo:FFN_HIDDEN + hi])
            act = (_silu(gt) * up).astype(BF16)
            y = y + _dot(act, w2_ref[lo:hi, :])
        out_ref[rows, :] = x + _rms(y, gpost_ref[...])


def _ffn(x2, layer, g_pre, g_post, w1, w2):
    t = x2.shape[0]
    tile = FFN_TOKEN_TILE
    tok = pl.BlockSpec((tile, D_MODEL), lambda i: (i, 0))
    return pl.pallas_call(
        _ffn_kernel,
        grid=(t // tile,),
        in_specs=[tok, _layer_spec((1, D_MODEL), layer), _layer_spec((1, D_MODEL), layer),
                  _layer_spec((D_MODEL, 2 * FFN_HIDDEN), layer),
                  _layer_spec((FFN_HIDDEN, D_MODEL), layer)],
        out_specs=tok,
        out_shape=jax.ShapeDtypeStruct((t, D_MODEL), F32),
        compiler_params=_compiler_params(1),
        name="ffn",
    )(x2, g_pre, g_post, w1, w2)


def kernel(x, norm_mix_pre, norm_mix_post, norm_ffn_pre, norm_ffn_post, w_in, attn_rel_bias, hgrn_lb_logits, hgrn_norm_g, gmlp_norm_g, gmlp_ws, gmlp_bs, lru_conv_w, lru_conv_b, lru_wa, lru_ba, lru_wx, lru_bx, lru_lambda, w_branch, w_out, w_ffn_in, w_ffn_out):
    b, s, d = x.shape
    x2 = x.reshape(b * s, d)
    w_mix_b, w_gate_b = _split_cast_w_in(w_in)
    w_branch_b = w_branch.astype(BF16)
    w_out_b = w_out.astype(BF16)
    w1_b = w_ffn_in.astype(BF16)
    w2_b = w_ffn_out.astype(BF16)
    row = lambda p: p.reshape(DEPTH, 1, p.shape[-1])
    g_mix_pre, g_mix_post = row(norm_mix_pre), row(norm_mix_post)
    g_ffn_pre, g_ffn_post = row(norm_ffn_pre), row(norm_ffn_post)
    bias_rows = _attn_bias_rows(attn_rel_bias)
    hg_consts = _hgrn_consts()
    hg_norm = row(hgrn_norm_g)
    gm_norm = row(gmlp_norm_g)
    gm_bias = jnp.repeat(jnp.swapaxes(gmlp_bs, 1, 2), HEAD_DIM, axis=2)
    lru_vec = jnp.concatenate(
        [lru_conv_w, lru_conv_b[:, None], lru_ba[:, None], lru_bx[:, None], lru_lambda[:, None]],
        axis=1)
    lru_w = _block_diag(jnp.stack([lru_wa, lru_wx], axis=1)).astype(BF16)
    for l in range(DEPTH):
        z_att, z_rest = _in_proj(x2, l, g_mix_pre, w_mix_b)
        z3 = z_rest.reshape(b, s, N_MIX_COLS - N_ATT_COLS)
        o_a = _attention(z_att.reshape(b, s, N_ATT_COLS), l, bias_rows)
        o_b = _hgrn2(z3, l, hgrn_lb_logits, hg_norm, *hg_consts)
        o_c = _gmlp(z3, l, gm_norm, gmlp_ws, gm_bias)
        o_d = _rg_lru(z3, l, lru_vec, lru_w)
        outs = [o.reshape(b * s, MIX_W) for o in (o_a, o_b, o_c, o_d)]
        x2 = _mix_out(x2, outs, l, g_mix_pre, g_mix_post, w_gate_b, w_branch_b, w_out_b)
        x2 = _ffn(x2, l, g_ffn_pre, g_ffn_post, w1_b, w2_b)
    return x2.reshape(b, s, d)
```

```python
import functools

import numpy as np
import jax
import jax.numpy as jnp
from jax import lax
from jax.experimental import pallas as pl
from jax.experimental.pallas import tpu as pltpu

D_MODEL = 1024
DEPTH = 4
CHUNK = 64
EPS = 1e-6
N_HEADS = 4
HEAD_DIM = 64
MIX_W = N_HEADS * HEAD_DIM
N_BRANCH = 4
ATT_LEFT_CHUNKS = 8
REL_MAX = 256
NEG_BIG = -1e30
LOG_FLOOR = 1e-30
GM_BLOCK = 128
CONV_W = 4
LRU_C = 8.0
FFN_HIDDEN = 2816
N_MIX_COLS = 11 * MIX_W
N_ATT_COLS = 3 * MIX_W
LOG2_E = 1.4426950408889634
ATT_Q_SCALE = HEAD_DIM ** -0.5 * LOG2_E

F32 = jnp.float32
BF16 = jnp.bfloat16

VMEM_LIMIT_BYTES = 56 * 1024 * 1024
TOKEN_TILE = 1024
FFN_TOKEN_TILE = 1024
SUB_TILE = 256
ATT_QB = 256
ATT_PAD = ATT_LEFT_CHUNKS * CHUNK
ATT_KB = ATT_PAD + ATT_QB
ATT_ROLL_W = 1024
MXU_TILE = 256
FFN_HIDDEN_CHUNKS = ((0, 6 * MXU_TILE), (6 * MXU_TILE, FFN_HIDDEN))
HG_CHUNK = 256
SUBLANES = 8
HG_BASE = 4
HG_LEVELS = (4, 8, 16, 32, 64, 128)


def _compiler_params(n_grid):
    return pltpu.CompilerParams(
        dimension_semantics=("arbitrary",) * n_grid,
        vmem_limit_bytes=VMEM_LIMIT_BYTES)


def _rms(x, g):
    return x * lax.rsqrt(jnp.mean(x * x, axis=-1, keepdims=True) + EPS) * g


def _gelu(x):
    c = 0.7978845608028654
    half_x = 0.5 * x
    return half_x + half_x * jnp.tanh(x * (c + (c * 0.044715) * (x * x)))


def _sigmoid(x):
    return 1.0 / (1.0 + jnp.exp(-x))


def _silu(x):
    return x * _sigmoid(x)


def _dot(a, b):
    return jnp.dot(a, b, preferred_element_type=F32)


def _dot_nt(a, b):
    return lax.dot_general(a, b, (((1,), (1,)), ((), ())), preferred_element_type=F32)


def _dot_tn(a, b):
    return lax.dot_general(a, b, (((0,), (0,)), ((), ())), preferred_element_type=F32)


def _split2(x):
    hi = x.astype(BF16)
    lo = (x - hi.astype(F32)).astype(BF16)
    return hi, lo


def _head_of_lane(shape, dim):
    return lax.broadcasted_iota(jnp.int32, shape, dim) // HEAD_DIM


def _layer_spec(shape, layer):
    zeros = (0,) * len(shape)
    return pl.BlockSpec((None,) + tuple(shape), lambda i: (layer,) + zeros,
                        pipeline_mode=pl.Buffered(1))


def _split_cast_kernel(w_ref, mix_ref, gate_ref):
    mix_ref[...] = w_ref[:, 0:N_MIX_COLS].astype(BF16)
    gate_ref[...] = w_ref[:, N_MIX_COLS:].astype(BF16)


def _split_cast_w_in(w_in):
    depth, d, n_cols = w_in.shape
    n_gate = n_cols - N_MIX_COLS
    rows = MXU_TILE
    blk = lambda n: pl.BlockSpec((None, rows, n), lambda l, i: (l, i, 0))
    return pl.pallas_call(
        _split_cast_kernel,
        grid=(depth, d // rows),
        in_specs=[blk(n_cols)],
        out_specs=[blk(N_MIX_COLS), blk(n_gate)],
        out_shape=[jax.ShapeDtypeStruct((depth, d, N_MIX_COLS), BF16),
                   jax.ShapeDtypeStruct((depth, d, n_gate), BF16)],
        compiler_params=_compiler_params(2),
        name="split_cast_w_in",
    )(w_in)


def _sub_tiles(n_rows):
    return tuple(pl.ds(r, SUB_TILE) for r in range(0, n_rows, SUB_TILE))


def _in_proj_kernel(x_ref, g_ref, w_ref, za_ref, zr_ref):
    for rows in _sub_tiles(x_ref.shape[0]):
        h = _rms(x_ref[rows, :], g_ref[...]).astype(BF16)
        z = _dot(h, w_ref[...])
        za_ref[rows, 0:MIX_W] = (z[:, 0:MIX_W] * ATT_Q_SCALE).astype(BF16)
        za_ref[rows, MIX_W:N_ATT_COLS] = z[:, MIX_W:N_ATT_COLS].astype(BF16)
        zr_ref[rows, :] = z[:, N_ATT_COLS:]


def _in_proj(x2, layer, g, w_mix):
    t = x2.shape[0]
    n_rest = N_MIX_COLS - N_ATT_COLS
    return pl.pallas_call(
        _in_proj_kernel,
        grid=(t // TOKEN_TILE,),
        in_specs=[
            pl.BlockSpec((TOKEN_TILE, D_MODEL), lambda i: (i, 0)),
            _layer_spec((1, D_MODEL), layer),
            _layer_spec((D_MODEL, N_MIX_COLS), layer),
        ],
        out_specs=[pl.BlockSpec((TOKEN_TILE, N_ATT_COLS), lambda i: (i, 0)),
                   pl.BlockSpec((TOKEN_TILE, n_rest), lambda i: (i, 0))],
        out_shape=[jax.ShapeDtypeStruct((t, N_ATT_COLS), BF16),
                   jax.ShapeDtypeStruct((t, n_rest), F32)],
        compiler_params=_compiler_params(1),
        name="in_proj",
    )(x2, g, w_mix)


def _attn_kernel(q_ref, k_ref, v_ref, brow_ref, o_ref, bias_scr):
    s_len = q_ref.shape[0]

    @pl.when(pl.program_id(0) == 0)
    def _():
        qi = lax.broadcasted_iota(jnp.int32, (ATT_QB, ATT_KB), 0) // CHUNK
        kc = lax.broadcasted_iota(jnp.int32, (ATT_QB, ATT_KB), 1) // CHUNK
        band = (kc >= qi) & (kc <= qi + ATT_LEFT_CHUNKS)
        for h in range(N_HEADS):
            rows = jnp.broadcast_to(brow_ref[h:h + 1, :], (ATT_QB, ATT_ROLL_W))
            tile = pltpu.roll(rows, ATT_ROLL_W - ATT_QB, 1, stride=1, stride_axis=0)
            bias_scr[h] = jnp.where(band, tile[:, :ATT_KB], NEG_BIG)

    lane_head = _head_of_lane((ATT_QB, MIX_W), 1)
    head_rows = [jnp.where(_head_of_lane((1, MIX_W), 1) == h, 1.0, 0.0).astype(BF16)
                 for h in range(N_HEADS)]

    def block(r0, k0, n_keys):
        q = q_ref[pl.ds(r0, ATT_QB), :]
        kb = k_ref[pl.ds(k0, n_keys), :]
        vb = v_ref[pl.ds(k0, n_keys), :]
        acc = jnp.zeros((ATT_QB, MIX_W), F32)
        for h in range(N_HEADS):
            s = _dot_nt(q * head_rows[h], kb) + bias_scr[h, :, pl.ds(ATT_KB - n_keys, n_keys)]
            m = jnp.max(s, axis=-1, keepdims=True)
            p = jnp.exp2(s - m)
            l = jnp.sum(p, axis=-1, keepdims=True)
            o = _dot(p.astype(BF16), vb)
            acc = acc + jnp.where(lane_head == h, o * (1.0 / l), 0.0)
        o_ref[pl.ds(r0, ATT_QB), :] = acc.astype(o_ref.dtype)

    n_head_blocks = ATT_PAD // ATT_QB
    for j in range(n_head_blocks):
        block(j * ATT_QB, 0, (j + 1) * ATT_QB)

    def body(j, carry):
        r0 = pl.multiple_of(j * ATT_QB, ATT_QB)
        block(r0, r0 - ATT_PAD, ATT_KB)
        return carry

    lax.fori_loop(n_head_blocks, s_len // ATT_QB, body, 0, unroll=6)


def _attn_bias_rows(rel_bias):
    dist = ATT_PAD + ATT_QB - np.arange(ATT_ROLL_W)
    idx = np.clip(dist, -(CHUNK - 1), REL_MAX) + (CHUNK - 1)
    return rel_bias.astype(F32)[:, :, idx] * LOG2_E


def _attention(za3, layer, bias_rows):
    b, s, _ = za3.shape
    col_spec = lambda c: pl.BlockSpec((None, s, MIX_W), lambda i, c=c: (i, 0, c))
    return pl.pallas_call(
        _attn_kernel,
        grid=(b,),
        in_specs=[col_spec(0), col_spec(1), col_spec(2),
                  _layer_spec((N_HEADS, ATT_ROLL_W), layer)],
        out_specs=pl.BlockSpec((None, s, MIX_W), lambda i: (i, 0, 0)),
        out_shape=jax.ShapeDtypeStruct((b, s, MIX_W), BF16),
        scratch_shapes=[pltpu.VMEM((N_HEADS, ATT_QB, ATT_KB), F32)],
        compiler_params=_compiler_params(1),
        name="attention",
    )(za3, za3, za3, bias_rows)


def _hgrn_half_rows(g, right):
    off = g if right else 0
    return np.concatenate([np.arange(2 * g * b + off, 2 * g * b + off + g)
                           for b in range(HG_CHUNK // (2 * g))])


def _hgrn_coef_matrix():
    r = np.arange(HG_CHUNK)[None, :]
    mats = [r <= np.arange(HG_CHUNK)[:, None]]
    for g in HG_LEVELS:
        for right in (True, False):
            t = _hgrn_half_rows(g, right)[:, None]
            bnd = (t // (2 * g)) * (2 * g) + g - 1
            mats.append((r > bnd) & (r <= t) if right else (r > t) & (r <= bnd))
    return np.concatenate(mats, axis=0).astype(np.float32)


def _hgrn_level_masks():
    half = HG_CHUNK // 2
    tc = np.arange(half)[:, None]
    sc = np.arange(2 * half)[None, :] % half
    return np.stack([(tc // g) == (sc // g) for g in HG_LEVELS]).astype(np.float32)


def _hgrn_consts():
    return (jnp.asarray(_hgrn_coef_matrix(), dtype=BF16), jnp.asarray(_hgrn_level_masks()))


def _hgrn_kernel(layer, q_ref, f_ref, i_ref, g_ref, lbl_ref, ng_ref, coef_ref, lmask_ref, o_ref):
    s_len = q_ref.shape[0]
    c = HG_CHUNK
    half = c // 2
    logits = lbl_ref[...]
    e = jnp.exp(logits - jnp.max(logits, axis=0, keepdims=True))
    p = e / jnp.sum(e, axis=0, keepdims=True)
    lb = jnp.zeros((1, MIX_W), F32)
    for l in range(1, layer + 1):
        lb = lb + p[l:l + 1, :]

    same_head = (_head_of_lane((MIX_W, MIX_W), 0) == _head_of_lane((MIX_W, MIX_W), 1))
    head_ones = jnp.where(same_head, 1.0, 0.0).astype(BF16)
    head_rows = [jnp.where(_head_of_lane((1, MIX_W), 1) == h, 1.0, 0.0).astype(BF16)
                 for h in range(N_HEADS)]
    group_shape = (c // SUBLANES, SUBLANES, MIX_W)
    row_in_group = lax.broadcasted_iota(jnp.int32, group_shape, 1) % HG_BASE

    def body(j, st):
        r0 = pl.multiple_of(j * c, c)
        fz = f_ref[pl.ds(r0, c), :]
        qf = _silu(q_ref[pl.ds(r0, c), :])
        sg = _sigmoid(fz)
        f = lb + (1.0 - lb) * sg
        lf = jnp.log2(jnp.maximum(f, LOG_FLOOR))
        kf = (1.0 - lb) * (1.0 - sg)
        v = i_ref[pl.ds(r0, c), :]
        vb = v.astype(BF16)

        lf_hi, lf_lo = _split2(lf)
        pc = _dot(coef_ref[0:c, :], lf_hi) + _dot(coef_ref[0:c, :], lf_lo)
        sums = _dot(coef_ref[c:, :], lf_hi)
        ptot = pc[c - 1:c, :]

        qd = (qf * jnp.exp2(pc)).astype(BF16)
        terms = [_dot_nt(qd, st.astype(BF16))]
        kd = (kf * jnp.exp2(ptot - pc)).astype(BF16)
        st_next = st * jnp.exp2(ptot) + jnp.where(same_head, _dot_tn(vb, kd), 0.0)

        p3 = pc.reshape(group_shape)
        q3, k3, v3 = (x.reshape(group_shape) for x in (qf, kf, v))

        def group_term(d):
            if d == 0:
                prod, vd = q3 * k3, v3
            else:
                e = jnp.exp2(jnp.minimum(p3 - pltpu.roll(p3, d, 1), 0.0))
                prod = jnp.where(row_in_group >= d, q3 * pltpu.roll(k3, d, 1) * e, 0.0)
                vd = pltpu.roll(v3, d, 1)
            w = _dot(prod.reshape(c, MIX_W).astype(BF16), head_ones)
            return w * vd.reshape(c, MIX_W)

        def level_term(li, g):
            n_blocks = half // g

            def half_rows(x, off):
                return jnp.concatenate(
                    [x[2 * g * b + off:2 * g * b + off + g] for b in range(n_blocks)], axis=0)

            cq = sums[li * c:li * c + half]
            ck = sums[li * c + half:(li + 1) * c]
            qt = (half_rows(qf, g) * jnp.exp2(cq)).astype(BF16)
            kt = (half_rows(kf, 0) * jnp.exp2(ck)).astype(BF16)
            vt = half_rows(v, 0).astype(BF16)
            out = jnp.zeros((half, MIX_W), F32)
            for h in range(0, N_HEADS, 2):
                kw = jnp.concatenate([kt * head_rows[h], kt * head_rows[h + 1]], axis=0)
                vw = jnp.concatenate([vt * head_rows[h], vt * head_rows[h + 1]], axis=0)
                sc = _dot_nt(qt, kw) * lmask_ref[li]
                out = out + _dot(sc.astype(BF16), vw)
            zero = jnp.zeros((g, MIX_W), F32)
            return jnp.concatenate(
                [piece for b in range(n_blocks) for piece in (zero, out[b * g:(b + 1) * g])], axis=0)

        for i in range(max(HG_BASE, len(HG_LEVELS))):
            if i < HG_BASE:
                terms.append(group_term(i))
            if i < len(HG_LEVELS):
                terms.append(level_term(i, HG_LEVELS[i]))
        while len(terms) > 1:
            terms = [a + b for a, b in zip(terms[0::2], terms[1::2])] + terms[len(terms) & ~1:]
        o = terms[0]

        ms = _dot((o * o).astype(BF16), head_ones) * (1.0 / HEAD_DIM)
        o = o * lax.rsqrt(ms + EPS) * ng_ref[...]
        o = o * _silu(g_ref[pl.ds(r0, c), :])
        o_ref[pl.ds(r0, c), :] = o.astype(o_ref.dtype)
        return st_next

    lax.fori_loop(0, s_len // c, body, jnp.zeros((MIX_W, MIX_W), F32), unroll=8)


def _hgrn2(z3, layer, lb_logits, norm_g, coef, level_masks):
    b, s, _ = z3.shape
    col_spec = lambda c: pl.BlockSpec((None, s, MIX_W), lambda i, c=c: (i, 0, c))
    return pl.pallas_call(
        functools.partial(_hgrn_kernel, layer),
        grid=(b,),
        in_specs=[col_spec(0), col_spec(1), col_spec(2), col_spec(3),
                  pl.BlockSpec((DEPTH, MIX_W), lambda i: (0, 0)),
                  _layer_spec((1, MIX_W), layer),
                  pl.BlockSpec(coef.shape, lambda i: (0, 0)),
                  pl.BlockSpec(level_masks.shape, lambda i: (0, 0, 0))],
        out_specs=pl.BlockSpec((None, s, MIX_W), lambda i: (i, 0, 0)),
        out_shape=jax.ShapeDtypeStruct((b, s, MIX_W), BF16),
        compiler_params=_compiler_params(1),
        name="hgrn2",
    )(z3, z3, z3, z3, lb_logits, norm_g, coef, level_masks)


def _gmlp_kernel(u_ref, v_ref, ng_ref, ws_ref, bias_ref, o_ref):
    s_len = u_ref.shape[0]
    prow = lax.broadcasted_iota(jnp.int32, (GM_BLOCK, GM_BLOCK), 0)
    pcol = lax.broadcasted_iota(jnp.int32, (GM_BLOCK, GM_BLOCK), 1)
    w_cat = jnp.concatenate(
        [jnp.where(pcol <= prow, ws_ref[g], 0.0).astype(BF16) for g in range(N_HEADS)], axis=1)
    group_rows = [jnp.where(_head_of_lane((1, MIX_W), 1) == g, 1.0, 0.0).astype(BF16)
                  for g in range(N_HEADS)]

    def body(j, carry):
        r0 = pl.multiple_of(j * GM_BLOCK, GM_BLOCK)
        vn = _rms(_gelu(v_ref[pl.ds(r0, GM_BLOCK), :]), ng_ref[...]).astype(BF16)
        vn_cat = jnp.concatenate([vn * group_rows[g] for g in range(N_HEADS)], axis=0)
        mixed = bias_ref[...] + _dot(w_cat, vn_cat)
        u = _gelu(u_ref[pl.ds(r0, GM_BLOCK), :])
        o_ref[pl.ds(r0, GM_BLOCK), :] = (u * mixed).astype(o_ref.dtype)
        return carry

    lax.fori_loop(0, s_len // GM_BLOCK, body, 0, unroll=2)


def _gmlp(z3, layer, norm_g, ws, bias):
    b, s, _ = z3.shape
    col_spec = lambda c: pl.BlockSpec((None, s, MIX_W), lambda i, c=c: (i, 0, c))
    return pl.pallas_call(
        _gmlp_kernel,
        grid=(b,),
        in_specs=[col_spec(4), col_spec(5),
                  _layer_spec((1, MIX_W), layer),
                  _layer_spec((N_HEADS, GM_BLOCK, GM_BLOCK), layer),
                  _layer_spec((GM_BLOCK, MIX_W), layer)],
        out_specs=pl.BlockSpec((None, s, MIX_W), lambda i: (i, 0, 0)),
        out_shape=jax.ShapeDtypeStruct((b, s, MIX_W), BF16),
        compiler_params=_compiler_params(1),
        name="gmlp",
    )(z3, z3, norm_g, ws, bias)


LRU_ROW_CONV_B, LRU_ROW_BA, LRU_ROW_BX, LRU_ROW_LAM = 4, 5, 6, 7


def _lru_kernel(x_ref, gate_ref, vec_ref, w_ref, o_ref, a_ref, b_ref):
    s_len = x_ref.shape[0]
    vrow = lambda k: vec_ref[k:k + 1, :]
    x = x_ref[...]
    t_head = lax.broadcasted_iota(jnp.int32, (SUBLANES, MIX_W), 0)

    def first_tile(fn, v):
        return jnp.concatenate([fn(v[0:SUBLANES]), v[SUBLANES:]], axis=0)

    xc = vrow(LRU_ROW_CONV_B) + x * vrow(CONV_W - 1)
    for d in range(1, CONV_W):
        xs = first_tile(lambda v, d=d: jnp.where(t_head >= d, v, 0.0), pltpu.roll(x, d, 0))
        xc = xc + xs * vrow(CONV_W - 1 - d)
    xcb = xc.astype(BF16)
    r = _sigmoid(_dot(xcb, w_ref[0]) + vrow(LRU_ROW_BA))
    ig = _sigmoid(_dot(xcb, w_ref[1]) + vrow(LRU_ROW_BX))
    nl = -vrow(LRU_ROW_LAM)
    softplus = jnp.maximum(nl, 0.0) + jnp.log(1.0 + jnp.exp(-jnp.abs(nl)))
    a_full = jnp.exp(r * (-LRU_C * softplus))
    mult = jnp.sqrt(jnp.maximum(1.0 - a_full * a_full, 0.0))
    mult = first_tile(lambda v: jnp.where(t_head == 0, 1.0, v), mult)
    tiles = (s_len // SUBLANES, SUBLANES, MIX_W)
    a = a_full.reshape(tiles)
    b = (mult * (ig * xc)).reshape(tiles)
    row_in_tile = lax.broadcasted_iota(jnp.int32, tiles, 1)
    d = 1
    while d < SUBLANES:
        ok = row_in_tile >= d
        a_prev = jnp.where(ok, pltpu.roll(a, d, 1), 1.0)
        b_prev = jnp.where(ok, pltpu.roll(b, d, 1), 0.0)
        b = a * b_prev + b
        a = a * a_prev
        d *= 2
    a_ref[...] = a.reshape(s_len, MIX_W)
    b_ref[...] = b.reshape(s_len, MIX_W)
    carry = jnp.zeros((1, MIX_W), F32)
    for r0 in range(0, s_len, SUBLANES):
        rows = pl.ds(r0, SUBLANES)
        h = b_ref[rows, :] + a_ref[rows, :] * carry
        b_ref[rows, :] = h
        carry = h[SUBLANES - 1:SUBLANES, :]
    o_ref[...] = (b_ref[...] * _gelu(gate_ref[...])).astype(o_ref.dtype)


def _block_diag(w):
    eye = jnp.eye(N_HEADS, dtype=w.dtype)
    bd = jnp.einsum('...hij,hg->...higj', w, eye)
    return bd.reshape(w.shape[:-3] + (MIX_W, MIX_W))


def _rg_lru(z3, layer, vec, w_bd):
    b, s, _ = z3.shape
    col_spec = lambda c: pl.BlockSpec((None, s, MIX_W), lambda i, c=c: (i, 0, c))
    return pl.pallas_call(
        _lru_kernel,
        grid=(b,),
        in_specs=[col_spec(6), col_spec(7),
                  _layer_spec((8, MIX_W), layer),
                  _layer_spec((2, MIX_W, MIX_W), layer)],
        out_specs=pl.BlockSpec((None, s, MIX_W), lambda i: (i, 0, 0)),
        out_shape=jax.ShapeDtypeStruct((b, s, MIX_W), BF16),
        scratch_shapes=[pltpu.VMEM((s, MIX_W), F32), pltpu.VMEM((s, MIX_W), F32)],
        compiler_params=_compiler_params(1),
        name="rg_lru",
    )(z3, z3, vec, w_bd)


def _mix_out_kernel(x_ref, oa_ref, ob_ref, oc_ref, od_ref, gpre_ref, gpost_ref,
                    wg_ref, wb_ref, wo_ref, out_ref):
    for rows in _sub_tiles(x_ref.shape[0]):
        x = x_ref[rows, :]
        h = _rms(x, gpre_ref[...]).astype(BF16)
        merged = jnp.zeros(x.shape, F32)
        for n, o_ref in enumerate((oa_ref, ob_ref, oc_ref, od_ref)):
            gate = _dot(h, wg_ref[:, n * D_MODEL:(n + 1) * D_MODEL])
            proj = _dot(o_ref[rows, :], wb_ref[n])
            merged = merged + _sigmoid(gate) * proj
        y = _dot(merged.astype(BF16), wo_ref[...])
        out_ref[rows, :] = x + _rms(y, gpost_ref[...])


def _mix_out(x2, outs, layer, g_pre, g_post, w_gate, w_branch, w_out):
    t = x2.shape[0]
    tok = lambda w: pl.BlockSpec((TOKEN_TILE, w), lambda i: (i, 0))
    return pl.pallas_call(
        _mix_out_kernel,
        grid=(t // TOKEN_TILE,),
        in_specs=[tok(D_MODEL), tok(MIX_W), tok(MIX_W), tok(MIX_W), tok(MIX_W),
                  _layer_spec((1, D_MODEL), layer), _layer_spec((1, D_MODEL), layer),
                  _layer_spec((D_MODEL, N_BRANCH * D_MODEL), layer),
                  _layer_spec((N_BRANCH, MIX_W, D_MODEL), layer),
                  _layer_spec((D_MODEL, D_MODEL), layer)],
        out_specs=tok(D_MODEL),
        out_shape=jax.ShapeDtypeStruct((t, D_MODEL), F32),
        compiler_params=_compiler_params(1),
        name="mix_out",
    )(x2, *outs, g_pre, g_post, w_gate, w_branch, w_out)


def _ffn_kernel(x_ref, gpre_ref, gpost_ref, w1_ref, w2_ref, out_ref):
    for rows in _sub_tiles(x_ref.shape[0]):
        x = x_ref[rows, :]
        h = _rms(x, gpre_ref[...]).astype(BF16)
        y = jnp.zeros(x.shape, F32)
        for lo, hi in FFN_HIDDEN_CHUNKS:
            gt = _dot(h, w1_ref[:, lo:hi])
            up = _dot(h, w1_ref[:, FFN_HIDDEN + lo:FFN_HIDDEN + hi])
            act = (_silu(gt) * up).astype(BF16)
            y = y + _dot(act, w2_ref[lo:hi, :])
        out_ref[rows, :] = x + _rms(y, gpost_ref[...])


def _ffn(x2, layer, g_pre, g_post, w1, w2):
    t = x2.shape[0]
    tile = FFN_TOKEN_TILE
    tok = pl.BlockSpec((tile, D_MODEL), lambda i: (i, 0))
    return pl.pallas_call(
        _ffn_kernel,
        grid=(t // tile,),
        in_specs=[tok, _layer_spec((1, D_MODEL), layer), _layer_spec((1, D_MODEL), layer),
                  _layer_spec((D_MODEL, 2 * FFN_HIDDEN), layer),
                  _layer_spec((FFN_HIDDEN, D_MODEL), layer)],
        out_specs=tok,
        out_shape=jax.ShapeDtypeStruct((t, D_MODEL), F32),
        compiler_params=_compiler_params(1),
        name="ffn",
    )(x2, g_pre, g_post, w1, w2)


def kernel(x, norm_mix_pre, norm_mix_post, norm_ffn_pre, norm_ffn_post, w_in, attn_rel_bias, hgrn_lb_logits, hgrn_norm_g, gmlp_norm_g, gmlp_ws, gmlp_bs, lru_conv_w, lru_conv_b, lru_wa, lru_ba, lru_wx, lru_bx, lru_lambda, w_branch, w_out, w_ffn_in, w_ffn_out):
    b, s, d = x.shape
    x2 = x.reshape(b * s, d)
    w_mix_b, w_gate_b = _split_cast_w_in(w_in)
    w_branch_b = w_branch.astype(BF16)
    w_out_b = w_out.astype(BF16)
    w1_b = w_ffn_in.astype(BF16)
    w2_b = w_ffn_out.astype(BF16)
    row = lambda p: p.reshape(DEPTH, 1, p.shape[-1])
    g_mix_pre, g_mix_post = row(norm_mix_pre), row(norm_mix_post)
    g_ffn_pre, g_ffn_post = row(norm_ffn_pre), row(norm_ffn_post)
    bias_rows = _attn_bias_rows(attn_rel_bias)
    hg_consts = _hgrn_consts()
    hg_norm = row(hgrn_norm_g)
    gm_norm = row(gmlp_norm_g)
    gm_bias = jnp.repeat(jnp.swapaxes(gmlp_bs, 1, 2), HEAD_DIM, axis=2)
    lru_vec = jnp.concatenate(
        [lru_conv_w, lru_conv_b[:, None], lru_ba[:, None], lru_bx[:, None], lru_lambda[:, None]],
        axis=1)
    lru_w = _block_diag(jnp.stack([lru_wa, lru_wx], axis=1)).astype(BF16)
    for l in range(DEPTH):
        z_att, z_rest = _in_proj(x2, l, g_mix_pre, w_mix_b)
        z3 = z_rest.reshape(b, s, N_MIX_COLS - N_ATT_COLS)
        o_a = _attention(z_att.reshape(b, s, N_ATT_COLS), l, bias_rows)
        o_b = _hgrn2(z3, l, hgrn_lb_logits, hg_norm, *hg_consts)
        o_c = _gmlp(z3, l, gm_norm, gmlp_ws, gm_bias)
        o_d = _rg_lru(z3, l, lru_vec, lru_w)
        outs = [o.reshape(b * s, MIX_W) for o in (o_a, o_b, o_c, o_d)]
        x2 = _mix_out(x2, outs, l, g_mix_pre, g_mix_post, w_gate_b, w_branch_b, w_out_b)
        x2 = _ffn(x2, l, g_ffn_pre, g_ffn_post, w1_b, w2_b)
    return x2.reshape(b, s, d)
```

```python
import functools

import numpy as np
import jax
import jax.numpy as jnp
from jax import lax
from jax.experimental import pallas as pl
from jax.experimental.pallas import tpu as pltpu

D_MODEL = 1024
DEPTH = 4
CHUNK = 64
EPS = 1e-6
N_HEADS = 4
HEAD_DIM = 64
MIX_W = N_HEADS * HEAD_DIM
N_BRANCH = 4
ATT_LEFT_CHUNKS = 8
REL_MAX = 256
NEG_BIG = -1e30
LOG_FLOOR = 1e-30
GM_BLOCK = 128
CONV_W = 4
LRU_C = 8.0
FFN_HIDDEN = 2816
N_MIX_COLS = 11 * MIX_W
N_ATT_COLS = 3 * MIX_W
LOG2_E = 1.4426950408889634
ATT_Q_SCALE = HEAD_DIM ** -0.5 * LOG2_E

F32 = jnp.float32
BF16 = jnp.bfloat16

VMEM_LIMIT_BYTES = 56 * 1024 * 1024
TOKEN_TILE = 1024
FFN_TOKEN_TILE = 1024
SUB_TILE = 256
ATT_QB = 256
ATT_PAD = ATT_LEFT_CHUNKS * CHUNK
ATT_KB = ATT_PAD + ATT_QB
ATT_ROLL_W = 1024
MXU_TILE = 256
FFN_HIDDEN_CHUNKS = ((0, 6 * MXU_TILE), (6 * MXU_TILE, FFN_HIDDEN))
HG_CHUNK = 256
SUBLANES = 8
HG_BASE = 4
HG_LEVELS = (4, 8, 16, 32, 64, 128)


def _compiler_params(n_grid):
    return pltpu.CompilerParams(
        dimension_semantics=("arbitrary",) * n_grid,
        vmem_limit_bytes=VMEM_LIMIT_BYTES)


def _rms(x, g):
    return x * lax.rsqrt(jnp.mean(x * x, axis=-1, keepdims=True) + EPS) * g


def _gelu(x):
    c = 0.7978845608028654
    half_x = 0.5 * x
    return half_x + half_x * jnp.tanh(x * (c + (c * 0.044715) * (x * x)))


def _sigmoid(x):
    return 1.0 / (1.0 + jnp.exp(-x))


def _silu(x):
    return x * _sigmoid(x)


def _dot(a, b):
    return jnp.dot(a, b, preferred_element_type=F32)


def _dot_nt(a, b):
    return lax.dot_general(a, b, (((1,), (1,)), ((), ())), preferred_element_type=F32)


def _dot_tn(a, b):
    return lax.dot_general(a, b, (((0,), (0,)), ((), ())), preferred_element_type=F32)


def _split2(x):
    hi = x.astype(BF16)
    lo = (x - hi.astype(F32)).astype(BF16)
    return hi, lo


def _head_of_lane(shape, dim):
    return lax.broadcasted_iota(jnp.int32, shape, dim) // HEAD_DIM


def _layer_spec(shape, layer):
    zeros = (0,) * len(shape)
    return pl.BlockSpec((None,) + tuple(shape), lambda i: (layer,) + zeros,
                        pipeline_mode=pl.Buffered(1))


def _split_cast_kernel(w_ref, mix_ref, gate_ref):
    mix_ref[...] = w_ref[:, 0:N_MIX_COLS].astype(BF16)
    gate_ref[...] = w_ref[:, N_MIX_COLS:].astype(BF16)


def _split_cast_w_in(w_in):
    depth, d, n_cols = w_in.shape
    n_gate = n_cols - N_MIX_COLS
    rows = MXU_TILE
    blk = lambda n: pl.BlockSpec((None, rows, n), lambda l, i: (l, i, 0))
    return pl.pallas_call(
        _split_cast_kernel,
        grid=(depth, d // rows),
        in_specs=[blk(n_cols)],
        out_specs=[blk(N_MIX_COLS), blk(n_gate)],
        out_shape=[jax.ShapeDtypeStruct((depth, d, N_MIX_COLS), BF16),
                   jax.ShapeDtypeStruct((depth, d, n_gate), BF16)],
        compiler_params=_compiler_params(2),
        name="split_cast_w_in",
    )(w_in)


def _sub_tiles(n_rows):
    return tuple(pl.ds(r, SUB_TILE) for r in range(0, n_rows, SUB_TILE))


def _in_proj_kernel(x_ref, g_ref, w_ref, za_ref, zr_ref):
    for rows in _sub_tiles(x_ref.shape[0]):
        h = _rms(x_ref[rows, :], g_ref[...]).astype(BF16)
        z = _dot(h, w_ref[...])
        za_ref[rows, 0:MIX_W] = (z[:, 0:MIX_W] * ATT_Q_SCALE).astype(BF16)
        za_ref[rows, MIX_W:N_ATT_COLS] = z[:, MIX_W:N_ATT_COLS].astype(BF16)
        zr_ref[rows, :] = z[:, N_ATT_COLS:]


def _in_proj(x2, layer, g, w_mix):
    t = x2.shape[0]
    n_rest = N_MIX_COLS - N_ATT_COLS
    return pl.pallas_call(
        _in_proj_kernel,
        grid=(t // TOKEN_TILE,),
        in_specs=[
            pl.BlockSpec((TOKEN_TILE, D_MODEL), lambda i: (i, 0)),
            _layer_spec((1, D_MODEL), layer),
            _layer_spec((D_MODEL, N_MIX_COLS), layer),
        ],
        out_specs=[pl.BlockSpec((TOKEN_TILE, N_ATT_COLS), lambda i: (i, 0)),
                   pl.BlockSpec((TOKEN_TILE, n_rest), lambda i: (i, 0))],
        out_shape=[jax.ShapeDtypeStruct((t, N_ATT_COLS), BF16),
                   jax.ShapeDtypeStruct((t, n_rest), F32)],
        compiler_params=_compiler_params(1),
        name="in_proj",
    )(x2, g, w_mix)


def _attn_kernel(q_ref, k_ref, v_ref, brow_ref, o_ref, bias_scr):
    s_len = q_ref.shape[0]

    @pl.when(pl.program_id(0) == 0)
    def _():
        qi = lax.broadcasted_iota(jnp.int32, (ATT_QB, ATT_KB), 0) // CHUNK
        kc = lax.broadcasted_iota(jnp.int32, (ATT_QB, ATT_KB), 1) // CHUNK
        band = (kc >= qi) & (kc <= qi + ATT_LEFT_CHUNKS)
        for h in range(N_HEADS):
            rows = jnp.broadcast_to(brow_ref[h:h + 1, :], (ATT_QB, ATT_ROLL_W))
            tile = pltpu.roll(rows, ATT_ROLL_W - ATT_QB, 1, stride=1, stride_axis=0)
            bias_scr[h] = jnp.where(band, tile[:, :ATT_KB], NEG_BIG)

    lane_head = _head_of_lane((ATT_QB, MIX_W), 1)
    head_rows = [jnp.where(_head_of_lane((1, MIX_W), 1) == h, 1.0, 0.0).astype(BF16)
                 for h in range(N_HEADS)]

    def block(r0, k0, n_keys):
        q = q_ref[pl.ds(r0, ATT_QB), :]
        kb = k_ref[pl.ds(k0, n_keys), :]
        vb = v_ref[pl.ds(k0, n_keys), :]
        acc = jnp.zeros((ATT_QB, MIX_W), F32)
        for h in range(N_HEADS):
            s = _dot_nt(q * head_rows[h], kb) + bias_scr[h, :, pl.ds(ATT_KB - n_keys, n_keys)]
            m = jnp.max(s, axis=-1, keepdims=True)
            p = jnp.exp2(s - m)
            l = jnp.sum(p, axis=-1, keepdims=True)
            o = _dot(p.astype(BF16), vb)
            acc = acc + jnp.where(lane_head == h, o * (1.0 / l), 0.0)
        o_ref[pl.ds(r0, ATT_QB), :] = acc.astype(o_ref.dtype)

    n_head_blocks = ATT_PAD // ATT_QB
    for j in range(n_head_blocks):
        block(j * ATT_QB, 0, (j + 1) * ATT_QB)

    def body(j, carry):
        r0 = pl.multiple_of(j * ATT_QB, ATT_QB)
        block(r0, r0 - ATT_PAD, ATT_KB)
        return carry

    lax.fori_loop(n_head_blocks, s_len // ATT_QB, body, 0, unroll=6)


def _attn_bias_rows(rel_bias):
    dist = ATT_PAD + ATT_QB - np.arange(ATT_ROLL_W)
    idx = np.clip(dist, -(CHUNK - 1), REL_MAX) + (CHUNK - 1)
    return rel_bias.astype(F32)[:, :, idx] * LOG2_E


def _attention(za3, layer, bias_rows):
    b, s, _ = za3.shape
    col_spec = lambda c: pl.BlockSpec((None, s, MIX_W), lambda i, c=c: (i, 0, c))
    return pl.pallas_call(
        _attn_kernel,
        grid=(b,),
        in_specs=[col_spec(0), col_spec(1), col_spec(2),
                  _layer_spec((N_HEADS, ATT_ROLL_W), layer)],
        out_specs=pl.BlockSpec((None, s, MIX_W), lambda i: (i, 0, 0)),
        out_shape=jax.ShapeDtypeStruct((b, s, MIX_W), BF16),
        scratch_shapes=[pltpu.VMEM((N_HEADS, ATT_QB, ATT_KB), F32)],
        compiler_params=_compiler_params(1),
        name="attention",
    )(za3, za3, za3, bias_rows)


def _hgrn_half_rows(g, right):
    off = g if right else 0
    return np.concatenate([np.arange(2 * g * b + off, 2 * g * b + off + g)
                           for b in range(HG_CHUNK // (2 * g))])


def _hgrn_coef_matrix():
    r = np.arange(HG_CHUNK)[None, :]
    mats = [r <= np.arange(HG_CHUNK)[:, None]]
    for g in HG_LEVELS:
        for right in (True, False):
            t = _hgrn_half_rows(g, right)[:, None]
            bnd = (t // (2 * g)) * (2 * g) + g - 1
            mats.append((r > bnd) & (r <= t) if right else (r > t) & (r <= bnd))
    return np.concatenate(mats, axis=0).astype(np.float32)


def _hgrn_level_masks():
    half = HG_CHUNK // 2
    tc = np.arange(half)[:, None]
    sc = np.arange(2 * half)[None, :] % half
    return np.stack([(tc // g) == (sc // g) for g in HG_LEVELS]).astype(np.float32)


def _hgrn_consts():
    return (jnp.asarray(_hgrn_coef_matrix(), dtype=BF16), jnp.asarray(_hgrn_level_masks()))


def _hgrn_gmlp_kernel(layer, q_ref, f_ref, i_ref, g_ref, gu_ref, gv_ref, lbl_ref, ng_ref,
                      coef_ref, lmask_ref, gm_ng_ref, gm_ws_ref, gm_bias_ref, o_ref, gm_o_ref):
    s_len = q_ref.shape[0]
    gm_w_cat = _gmlp_weights(gm_ws_ref)
    c = HG_CHUNK
    half = c // 2
    logits = lbl_ref[...]
    e = jnp.exp(logits - jnp.max(logits, axis=0, keepdims=True))
    p = e / jnp.sum(e, axis=0, keepdims=True)
    lb = jnp.zeros((1, MIX_W), F32)
    for l in range(1, layer + 1):
        lb = lb + p[l:l + 1, :]

    same_head = (_head_of_lane((MIX_W, MIX_W), 0) == _head_of_lane((MIX_W, MIX_W), 1))
    head_ones = jnp.where(same_head, 1.0, 0.0).astype(BF16)
    head_rows = [jnp.where(_head_of_lane((1, MIX_W), 1) == h, 1.0, 0.0).astype(BF16)
                 for h in range(N_HEADS)]
    group_shape = (c // SUBLANES, SUBLANES, MIX_W)
    row_in_group = lax.broadcasted_iota(jnp.int32, group_shape, 1) % HG_BASE

    def body(j, st):
        r0 = pl.multiple_of(j * c, c)
        fz = f_ref[pl.ds(r0, c), :]
        qf = _silu(q_ref[pl.ds(r0, c), :])
        sg = _sigmoid(fz)
        f = lb + (1.0 - lb) * sg
        lf = jnp.log2(jnp.maximum(f, LOG_FLOOR))
        kf = (1.0 - lb) * (1.0 - sg)
        v = i_ref[pl.ds(r0, c), :]
        vb = v.astype(BF16)

        lf_hi, lf_lo = _split2(lf)
        pc = _dot(coef_ref[0:c, :], lf_hi) + _dot(coef_ref[0:c, :], lf_lo)
        sums = _dot(coef_ref[c:, :], lf_hi)
        ptot = pc[c - 1:c, :]

        qd = (qf * jnp.exp2(pc)).astype(BF16)
        terms = [_dot_nt(qd, st.astype(BF16))]
        kd = (kf * jnp.exp2(ptot - pc)).astype(BF16)
        st_next = st * jnp.exp2(ptot) + jnp.where(same_head, _dot_tn(vb, kd), 0.0)

        p3 = pc.reshape(group_shape)
        q3, k3, v3 = (x.reshape(group_shape) for x in (qf, kf, v))

        def group_term(d):
            if d == 0:
                prod, vd = q3 * k3, v3
            else:
                e = jnp.exp2(jnp.minimum(p3 - pltpu.roll(p3, d, 1), 0.0))
                prod = jnp.where(row_in_group >= d, q3 * pltpu.roll(k3, d, 1) * e, 0.0)
                vd = pltpu.roll(v3, d, 1)
            w = _dot(prod.reshape(c, MIX_W).astype(BF16), head_ones)
            return w * vd.reshape(c, MIX_W)

        def level_term(li, g):
            n_blocks = half // g

            def half_rows(x, off):
                return jnp.concatenate(
                    [x[2 * g * b + off:2 * g * b + off + g] for b in range(n_blocks)], axis=0)

            cq = sums[li * c:li * c + half]
            ck = sums[li * c + half:(li + 1) * c]
            qt = (half_rows(qf, g) * jnp.exp2(cq)).astype(BF16)
            kt = (half_rows(kf, 0) * jnp.exp2(ck)).astype(BF16)
            vt = half_rows(v, 0).astype(BF16)
            out = jnp.zeros((half, MIX_W), F32)
            for h in range(0, N_HEADS, 2):
                kw = jnp.concatenate([kt * head_rows[h], kt * head_rows[h + 1]], axis=0)
                vw = jnp.concatenate([vt * head_rows[h], vt * head_rows[h + 1]], axis=0)
                sc = _dot_nt(qt, kw) * lmask_ref[li]
                out = out + _dot(sc.astype(BF16), vw)
            zero = jnp.zeros((g, MIX_W), F32)
            return jnp.concatenate(
                [piece for b in range(n_blocks) for piece in (zero, out[b * g:(b + 1) * g])], axis=0)

        gm_blocks = list(range(0, c, GM_BLOCK))
        for i in range(max(HG_BASE, len(HG_LEVELS))):
            if i < HG_BASE:
                terms.append(group_term(i))
            if i < len(HG_LEVELS):
                terms.append(level_term(i, HG_LEVELS[i]))
            if i % 2 == 1 and gm_blocks:
                _gmlp_block(r0 + gm_blocks.pop(0), gu_ref, gv_ref, gm_ng_ref, gm_bias_ref,
                            gm_o_ref, gm_w_cat, head_rows)
        while len(terms) > 1:
            terms = [a + b for a, b in zip(terms[0::2], terms[1::2])] + terms[len(terms) & ~1:]
        o = terms[0]

        ms = _dot((o * o).astype(BF16), head_ones) * (1.0 / HEAD_DIM)
        o = o * lax.rsqrt(ms + EPS) * ng_ref[...]
        o = o * _silu(g_ref[pl.ds(r0, c), :])
        o_ref[pl.ds(r0, c), :] = o.astype(o_ref.dtype)
        return st_next

    lax.fori_loop(0, s_len // c, body, jnp.zeros((MIX_W, MIX_W), F32), unroll=8)


def _hgrn2_gmlp(z3, layer, lb_logits, hg_norm, coef, level_masks, gm_norm, gm_ws, gm_bias):
    b, s, _ = z3.shape
    col_spec = lambda c: pl.BlockSpec((None, s, MIX_W), lambda i, c=c: (i, 0, c))
    out_spec = pl.BlockSpec((None, s, MIX_W), lambda i: (i, 0, 0))
    out_shape = jax.ShapeDtypeStruct((b, s, MIX_W), BF16)
    return pl.pallas_call(
        functools.partial(_hgrn_gmlp_kernel, layer),
        grid=(b,),
        in_specs=[col_spec(0), col_spec(1), col_spec(2), col_spec(3), col_spec(4), col_spec(5),
                  pl.BlockSpec((DEPTH, MIX_W), lambda i: (0, 0)),
                  _layer_spec((1, MIX_W), layer),
                  pl.BlockSpec(coef.shape, lambda i: (0, 0)),
                  pl.BlockSpec(level_masks.shape, lambda i: (0, 0, 0)),
                  _layer_spec((1, MIX_W), layer),
                  _layer_spec((N_HEADS, GM_BLOCK, GM_BLOCK), layer),
                  _layer_spec((GM_BLOCK, MIX_W), layer)],
        out_specs=[out_spec, out_spec],
        out_shape=[out_shape, out_shape],
        compiler_params=_compiler_params(1),
        name="hgrn2_gmlp",
    )(z3, z3, z3, z3, z3, z3, lb_logits, hg_norm, coef, level_masks, gm_norm, gm_ws, gm_bias)


def _gmlp_weights(ws_ref):
    prow = lax.broadcasted_iota(jnp.int32, (GM_BLOCK, GM_BLOCK), 0)
    pcol = lax.broadcasted_iota(jnp.int32, (GM_BLOCK, GM_BLOCK), 1)
    return jnp.concatenate(
        [jnp.where(pcol <= prow, ws_ref[g], 0.0).astype(BF16) for g in range(N_HEADS)], axis=1)


def _gmlp_block(r0, u_ref, v_ref, ng_ref, bias_ref, o_ref, w_cat, group_rows):
    rows = pl.ds(r0, GM_BLOCK)
    vn = _rms(_gelu(v_ref[rows, :]), ng_ref[...]).astype(BF16)
    vn_cat = jnp.concatenate([vn * group_rows[g] for g in range(N_HEADS)], axis=0)
    mixed = bias_ref[...] + _dot(w_cat, vn_cat)
    o_ref[rows, :] = (_gelu(u_ref[rows, :]) * mixed).astype(o_ref.dtype)


LRU_ROW_CONV_B, LRU_ROW_BA, LRU_ROW_BX, LRU_ROW_LAM = 4, 5, 6, 7


def _lru_kernel(x_ref, gate_ref, vec_ref, w_ref, o_ref, a_ref, b_ref):
    s_len = x_ref.shape[0]
    vrow = lambda k: vec_ref[k:k + 1, :]
    x = x_ref[...]
    t_head = lax.broadcasted_iota(jnp.int32, (SUBLANES, MIX_W), 0)

    def first_tile(fn, v):
        return jnp.concatenate([fn(v[0:SUBLANES]), v[SUBLANES:]], axis=0)

    xc = vrow(LRU_ROW_CONV_B) + x * vrow(CONV_W - 1)
    for d in range(1, CONV_W):
        xs = first_tile(lambda v, d=d: jnp.where(t_head >= d, v, 0.0), pltpu.roll(x, d, 0))
        xc = xc + xs * vrow(CONV_W - 1 - d)
    xcb = xc.astype(BF16)
    r = _sigmoid(_dot(xcb, w_ref[0]) + vrow(LRU_ROW_BA))
    ig = _sigmoid(_dot(xcb, w_ref[1]) + vrow(LRU_ROW_BX))
    nl = -vrow(LRU_ROW_LAM)
    softplus = jnp.maximum(nl, 0.0) + jnp.log(1.0 + jnp.exp(-jnp.abs(nl)))
    a_full = jnp.exp(r * (-LRU_C * softplus))
    mult = jnp.sqrt(jnp.maximum(1.0 - a_full * a_full, 0.0))
    mult = first_tile(lambda v: jnp.where(t_head == 0, 1.0, v), mult)
    tiles = (s_len // SUBLANES, SUBLANES, MIX_W)
    a = a_full.reshape(tiles)
    b = (mult * (ig * xc)).reshape(tiles)
    row_in_tile = lax.broadcasted_iota(jnp.int32, tiles, 1)
    d = 1
    while d < SUBLANES:
        ok = row_in_tile >= d
        a_prev = jnp.where(ok, pltpu.roll(a, d, 1), 1.0)
        b_prev = jnp.where(ok, pltpu.roll(b, d, 1), 0.0)
        b = a * b_prev + b
        a = a * a_prev
        d *= 2
    a_ref[...] = a.reshape(s_len, MIX_W)
    b_ref[...] = b.reshape(s_len, MIX_W)
    carry = jnp.zeros((1, MIX_W), F32)
    for r0 in range(0, s_len, SUBLANES):
        rows = pl.ds(r0, SUBLANES)
        h = b_ref[rows, :] + a_ref[rows, :] * carry
        b_ref[rows, :] = h
        carry = h[SUBLANES - 1:SUBLANES, :]
    o_ref[...] = (b_ref[...] * _gelu(gate_ref[...])).astype(o_ref.dtype)


def _block_diag(w):
    eye = jnp.eye(N_HEADS, dtype=w.dtype)
    bd = jnp.einsum('...hij,hg->...higj', w, eye)
    return bd.reshape(w.shape[:-3] + (MIX_W, MIX_W))


def _rg_lru(z3, layer, vec, w_bd):
    b, s, _ = z3.shape
    col_spec = lambda c: pl.BlockSpec((None, s, MIX_W), lambda i, c=c: (i, 0, c))
    return pl.pallas_call(
        _lru_kernel,
        grid=(b,),
        in_specs=[col_spec(6), col_spec(7),
                  _layer_spec((8, MIX_W), layer),
                  _layer_spec((2, MIX_W, MIX_W), layer)],
        out_specs=pl.BlockSpec((None, s, MIX_W), lambda i: (i, 0, 0)),
        out_shape=jax.ShapeDtypeStruct((b, s, MIX_W), BF16),
        scratch_shapes=[pltpu.VMEM((s, MIX_W), F32), pltpu.VMEM((s, MIX_W), F32)],
        compiler_params=_compiler_params(1),
        name="rg_lru",
    )(z3, z3, vec, w_bd)


def _mix_out_kernel(x_ref, oa_ref, ob_ref, oc_ref, od_ref, gpre_ref, gpost_ref,
                    wg_ref, wb_ref, wo_ref, out_ref):
    for rows in _sub_tiles(x_ref.shape[0]):
        x = x_ref[rows, :]
        h = _rms(x, gpre_ref[...]).astype(BF16)
        merged = jnp.zeros(x.shape, F32)
        for n, o_ref in enumerate((oa_ref, ob_ref, oc_ref, od_ref)):
            gate = _dot(h, wg_ref[:, n * D_MODEL:(n + 1) * D_MODEL])
            proj = _dot(o_ref[rows, :], wb_ref[n])
            merged = merged + _sigmoid(gate) * proj
        y = _dot(merged.astype(BF16), wo_ref[...])
        out_ref[rows, :] = x + _rms(y, gpost_ref[...])


def _mix_out(x2, outs, layer, g_pre, g_post, w_gate, w_branch, w_out):
    t = x2.shape[0]
    tok = lambda w: pl.BlockSpec((TOKEN_TILE, w), lambda i: (i, 0))
    return pl.pallas_call(
        _mix_out_kernel,
        grid=(t // TOKEN_TILE,),
        in_specs=[tok(D_MODEL), tok(MIX_W), tok(MIX_W), tok(MIX_W), tok(MIX_W),
                  _layer_spec((1, D_MODEL), layer), _layer_spec((1, D_MODEL), layer),
                  _layer_spec((D_MODEL, N_BRANCH * D_MODEL), layer),
                  _layer_spec((N_BRANCH, MIX_W, D_MODEL), layer),
                  _layer_spec((D_MODEL, D_MODEL), layer)],
        out_specs=tok(D_MODEL),
        out_shape=jax.ShapeDtypeStruct((t, D_MODEL), F32),
        compiler_params=_compiler_params(1),
        name="mix_out",
    )(x2, *outs, g_pre, g_post, w_gate, w_branch, w_out)


def _ffn_kernel(x_ref, gpre_ref, gpost_ref, w1_ref, w2_ref, out_ref):
    for rows in _sub_tiles(x_ref.shape[0]):
        x = x_ref[rows, :]
        h = _rms(x, gpre_ref[...]).astype(BF16)
        y = jnp.zeros(x.shape, F32)
        for lo, hi in FFN_HIDDEN_CHUNKS:
            gt = _dot(h, w1_ref[:, lo:hi])
            up = _dot(h, w1_ref[:, FFN_HIDDEN + lo:FFN_HIDDEN + hi])
            act = (_silu(gt) * up).astype(BF16)
            y = y + _dot(act, w2_ref[lo:hi, :])
        out_ref[rows, :] = x + _rms(y, gpost_ref[...])


def _ffn(x2, layer, g_pre, g_post, w1, w2):
    t = x2.shape[0]
    tile = FFN_TOKEN_TILE
    tok = pl.BlockSpec((tile, D_MODEL), lambda i: (i, 0))
    return pl.pallas_call(
        _ffn_kernel,
        grid=(t // tile,),
        in_specs=[tok, _layer_spec((1, D_MODEL), layer), _layer_spec((1, D_MODEL), layer),
                  _layer_spec((D_MODEL, 2 * FFN_HIDDEN), layer),
                  _layer_spec((FFN_HIDDEN, D_MODEL), layer)],
        out_specs=tok,
        out_shape=jax.ShapeDtypeStruct((t, D_MODEL), F32),
        compiler_params=_compiler_params(1),
        name="ffn",
    )(x2, g_pre, g_post, w1, w2)


def kernel(x, norm_mix_pre, norm_mix_post, norm_ffn_pre, norm_ffn_post, w_in, attn_rel_bias, hgrn_lb_logits, hgrn_norm_g, gmlp_norm_g, gmlp_ws, gmlp_bs, lru_conv_w, lru_conv_b, lru_wa, lru_ba, lru_wx, lru_bx, lru_lambda, w_branch, w_out, w_ffn_in, w_ffn_out):
    b, s, d = x.shape
    x2 = x.reshape(b * s, d)
    w_mix_b, w_gate_b = _split_cast_w_in(w_in)
    w_branch_b = w_branch.astype(BF16)
    w_out_b = w_out.astype(BF16)
    w1_b = w_ffn_in.astype(BF16)
    w2_b = w_ffn_out.astype(BF16)
    row = lambda p: p.reshape(DEPTH, 1, p.shape[-1])
    g_mix_pre, g_mix_post = row(norm_mix_pre), row(norm_mix_post)
    g_ffn_pre, g_ffn_post = row(norm_ffn_pre), row(norm_ffn_post)
    bias_rows = _attn_bias_rows(attn_rel_bias)
    hg_consts = _hgrn_consts()
    hg_norm = row(hgrn_norm_g)
    gm_norm = row(gmlp_norm_g)
    gm_bias = jnp.repeat(jnp.swapaxes(gmlp_bs, 1, 2), HEAD_DIM, axis=2)
    lru_vec = jnp.concatenate(
        [lru_conv_w, lru_conv_b[:, None], lru_ba[:, None], lru_bx[:, None], lru_lambda[:, None]],
        axis=1)
    lru_w = _block_diag(jnp.stack([lru_wa, lru_wx], axis=1)).astype(BF16)
    for l in range(DEPTH):
        z_att, z_rest = _in_proj(x2, l, g_mix_pre, w_mix_b)
        z3 = z_rest.reshape(b, s, N_MIX_COLS - N_ATT_COLS)
        o_a = _attention(z_att.reshape(b, s, N_ATT_COLS), l, bias_rows)
        o_b, o_c = _hgrn2_gmlp(z3, l, hgrn_lb_logits, hg_norm, *hg_consts, gm_norm, gmlp_ws, gm_bias)
        o_d = _rg_lru(z3, l, lru_vec, lru_w)
        outs = [o.reshape(b * s, MIX_W) for o in (o_a, o_b, o_c, o_d)]
        x2 = _mix_out(x2, outs, l, g_mix_pre, g_mix_post, w_gate_b, w_branch_b, w_out_b)
        x2 = _ffn(x2, l, g_ffn_pre, g_ffn_post, w1_b, w2_b)
    return x2.reshape(b, s, d)
```

```python
import functools

import numpy as np
import jax
import jax.numpy as jnp
from jax import lax
from jax.experimental import pallas as pl
from jax.experimental.pallas import tpu as pltpu

D_MODEL = 1024
DEPTH = 4
CHUNK = 64
EPS = 1e-6
N_HEADS = 4
HEAD_DIM = 64
MIX_W = N_HEADS * HEAD_DIM
N_BRANCH = 4
ATT_LEFT_CHUNKS = 8
REL_MAX = 256
NEG_BIG = -1e30
LOG_FLOOR = 1e-30
GM_BLOCK = 128
CONV_W = 4
LRU_C = 8.0
FFN_HIDDEN = 2816
N_MIX_COLS = 11 * MIX_W
N_ATT_COLS = 3 * MIX_W
LOG2_E = 1.4426950408889634
ATT_Q_SCALE = HEAD_DIM ** -0.5 * LOG2_E

F32 = jnp.float32
BF16 = jnp.bfloat16

VMEM_LIMIT_BYTES = 56 * 1024 * 1024
TOKEN_TILE = 1024
FFN_TOKEN_TILE = 1024
SUB_TILE = 256
ATT_QB = 256
ATT_PAD = ATT_LEFT_CHUNKS * CHUNK
ATT_KB = ATT_PAD + ATT_QB
ATT_ROLL_W = 1024
MXU_TILE = 256
FFN_HIDDEN_CHUNKS = ((0, 6 * MXU_TILE), (6 * MXU_TILE, FFN_HIDDEN))
HG_CHUNK = 256
SUBLANES = 8
HG_BASE = 4
HG_LEVELS = (4, 8, 16, 32, 64, 128)


def _compiler_params(n_grid):
    return pltpu.CompilerParams(
        dimension_semantics=("arbitrary",) * n_grid,
        vmem_limit_bytes=VMEM_LIMIT_BYTES)


def _rms(x, g):
    return x * lax.rsqrt(jnp.mean(x * x, axis=-1, keepdims=True) + EPS) * g


def _gelu(x):
    c = 0.7978845608028654
    half_x = 0.5 * x
    return half_x + half_x * jnp.tanh(x * (c + (c * 0.044715) * (x * x)))


def _sigmoid(x):
    return 1.0 / (1.0 + jnp.exp(-x))


def _silu(x):
    return x * _sigmoid(x)


def _dot(a, b):
    return jnp.dot(a, b, preferred_element_type=F32)


def _dot_nt(a, b):
    return lax.dot_general(a, b, (((1,), (1,)), ((), ())), preferred_element_type=F32)


def _dot_tn(a, b):
    return lax.dot_general(a, b, (((0,), (0,)), ((), ())), preferred_element_type=F32)


def _split2(x):
    hi = x.astype(BF16)
    lo = (x - hi.astype(F32)).astype(BF16)
    return hi, lo


def _head_of_lane(shape, dim):
    return lax.broadcasted_iota(jnp.int32, shape, dim) // HEAD_DIM


def _resident_spec(shape):
    zeros = (0,) * len(shape)
    return pl.BlockSpec(tuple(shape), lambda i: zeros, pipeline_mode=pl.Buffered(1))


def _cast_specs(jobs, n_steps):
    in_specs, out_specs, out_shapes = [], [], []
    for stack, layer, col_ranges in jobs:
        _, n_rows, n_cols = stack.shape
        slab = n_rows // n_steps
        in_specs.append(pl.BlockSpec((None, slab, n_cols), lambda i, layer=layer: (layer, i, 0)))
        for lo, hi in col_ranges:
            out_specs.append(pl.BlockSpec((slab, hi - lo), lambda i: (i, 0)))
            out_shapes.append(jax.ShapeDtypeStruct((n_rows, hi - lo), BF16))
    return in_specs, out_specs, out_shapes


def _cast_slabs(col_ranges_per_job, src_refs, dst_refs):
    dst_refs = iter(dst_refs)
    for col_ranges, src_ref in zip(col_ranges_per_job, src_refs):
        for lo, hi in col_ranges:
            next(dst_refs)[...] = src_ref[:, lo:hi].astype(BF16)


def _layer_spec(shape, layer):
    zeros = (0,) * len(shape)
    return pl.BlockSpec((None,) + tuple(shape), lambda i: (layer,) + zeros,
                        pipeline_mode=pl.Buffered(1))


def _split_cast_kernel(w_ref, mix_ref, gate_ref):
    mix_ref[...] = w_ref[:, 0:N_MIX_COLS].astype(BF16)
    gate_ref[...] = w_ref[:, N_MIX_COLS:].astype(BF16)


def _split_cast_w_in(w_in, layer):
    _, d, n_cols = w_in.shape
    n_gate = n_cols - N_MIX_COLS
    rows = MXU_TILE
    blk = lambda n: pl.BlockSpec((rows, n), lambda i: (i, 0))
    return pl.pallas_call(
        _split_cast_kernel,
        grid=(d // rows,),
        in_specs=[pl.BlockSpec((None, rows, n_cols), lambda i: (layer, i, 0))],
        out_specs=[blk(N_MIX_COLS), blk(n_gate)],
        out_shape=[jax.ShapeDtypeStruct((d, N_MIX_COLS), BF16),
                   jax.ShapeDtypeStruct((d, n_gate), BF16)],
        compiler_params=_compiler_params(1),
        name="split_cast_w_in",
    )(w_in)


def _sub_tiles(n_rows):
    return tuple(pl.ds(r, SUB_TILE) for r in range(0, n_rows, SUB_TILE))


def _in_proj_kernel(cast_cols, x_ref, g_ref, w_ref, *refs):
    n_cast = len(cast_cols)
    za_ref, zr_ref = refs[n_cast:n_cast + 2]
    _cast_slabs(cast_cols, refs[:n_cast], refs[n_cast + 2:])
    for rows in _sub_tiles(x_ref.shape[0]):
        h = _rms(x_ref[rows, :], g_ref[...]).astype(BF16)
        z = _dot(h, w_ref[...])
        za_ref[rows, 0:MIX_W] = (z[:, 0:MIX_W] * ATT_Q_SCALE).astype(BF16)
        za_ref[rows, MIX_W:N_ATT_COLS] = z[:, MIX_W:N_ATT_COLS].astype(BF16)
        zr_ref[rows, :] = z[:, N_ATT_COLS:]


def _in_proj(x2, layer, g, w_mix, cast_jobs=()):
    t = x2.shape[0]
    n_rest = N_MIX_COLS - N_ATT_COLS
    n_steps = t // TOKEN_TILE
    cast_in, cast_out, cast_shapes = _cast_specs(cast_jobs, n_steps)
    return pl.pallas_call(
        functools.partial(_in_proj_kernel, tuple(job[2] for job in cast_jobs)),
        grid=(n_steps,),
        in_specs=[
            pl.BlockSpec((TOKEN_TILE, D_MODEL), lambda i: (i, 0)),
            _layer_spec((1, D_MODEL), layer),
            _resident_spec(w_mix.shape),
        ] + cast_in,
        out_specs=[pl.BlockSpec((TOKEN_TILE, N_ATT_COLS), lambda i: (i, 0)),
                   pl.BlockSpec((TOKEN_TILE, n_rest), lambda i: (i, 0))] + cast_out,
        out_shape=[jax.ShapeDtypeStruct((t, N_ATT_COLS), BF16),
                   jax.ShapeDtypeStruct((t, n_rest), F32)] + cast_shapes,
        compiler_params=_compiler_params(1),
        name="in_proj",
    )(x2, g, w_mix, *(job[0] for job in cast_jobs))


def _attn_kernel(q_ref, k_ref, v_ref, brow_ref, o_ref, bias_scr):
    s_len = q_ref.shape[0]

    @pl.when(pl.program_id(0) == 0)
    def _():
        qi = lax.broadcasted_iota(jnp.int32, (ATT_QB, ATT_KB), 0) // CHUNK
        kc = lax.broadcasted_iota(jnp.int32, (ATT_QB, ATT_KB), 1) // CHUNK
        band = (kc >= qi) & (kc <= qi + ATT_LEFT_CHUNKS)
        for h in range(N_HEADS):
            rows = jnp.broadcast_to(brow_ref[h:h + 1, :], (ATT_QB, ATT_ROLL_W))
            tile = pltpu.roll(rows, ATT_ROLL_W - ATT_QB, 1, stride=1, stride_axis=0)
            bias_scr[h] = jnp.where(band, tile[:, :ATT_KB], NEG_BIG)

    lane_head = _head_of_lane((ATT_QB, MIX_W), 1)
    head_rows = [jnp.where(_head_of_lane((1, MIX_W), 1) == h, 1.0, 0.0).astype(BF16)
                 for h in range(N_HEADS)]

    def block(r0, k0, n_keys):
        q = q_ref[pl.ds(r0, ATT_QB), :]
        kb = k_ref[pl.ds(k0, n_keys), :]
        vb = v_ref[pl.ds(k0, n_keys), :]
        acc = jnp.zeros((ATT_QB, MIX_W), F32)
        for h in range(N_HEADS):
            s = _dot_nt(q * head_rows[h], kb) + bias_scr[h, :, pl.ds(ATT_KB - n_keys, n_keys)]
            m = jnp.max(s, axis=-1, keepdims=True)
            p = jnp.exp2(s - m)
            l = jnp.sum(p, axis=-1, keepdims=True)
            o = _dot(p.astype(BF16), vb)
            acc = acc + jnp.where(lane_head == h, o * (1.0 / l), 0.0)
        o_ref[pl.ds(r0, ATT_QB), :] = acc.astype(o_ref.dtype)

    n_head_blocks = ATT_PAD // ATT_QB
    for j in range(n_head_blocks):
        block(j * ATT_QB, 0, (j + 1) * ATT_QB)

    def body(j, carry):
        r0 = pl.multiple_of(j * ATT_QB, ATT_QB)
        block(r0, r0 - ATT_PAD, ATT_KB)
        return carry

    lax.fori_loop(n_head_blocks, s_len // ATT_QB, body, 0, unroll=6)


def _attn_bias_rows(rel_bias):
    dist = ATT_PAD + ATT_QB - np.arange(ATT_ROLL_W)
    idx = np.clip(dist, -(CHUNK - 1), REL_MAX) + (CHUNK - 1)
    return rel_bias.astype(F32)[:, :, idx] * LOG2_E


def _attention(za3, layer, bias_rows):
    b, s, _ = za3.shape
    col_spec = lambda c: pl.BlockSpec((None, s, MIX_W), lambda i, c=c: (i, 0, c))
    return pl.pallas_call(
        _attn_kernel,
        grid=(b,),
        in_specs=[col_spec(0), col_spec(1), col_spec(2),
                  _layer_spec((N_HEADS, ATT_ROLL_W), layer)],
        out_specs=pl.BlockSpec((None, s, MIX_W), lambda i: (i, 0, 0)),
        out_shape=jax.ShapeDtypeStruct((b, s, MIX_W), BF16),
        scratch_shapes=[pltpu.VMEM((N_HEADS, ATT_QB, ATT_KB), F32)],
        compiler_params=_compiler_params(1),
        name="attention",
    )(za3, za3, za3, bias_rows)


def _hgrn_half_rows(g, right):
    off = g if right else 0
    return np.concatenate([np.arange(2 * g * b + off, 2 * g * b + off + g)
                           for b in range(HG_CHUNK // (2 * g))])


def _hgrn_coef_matrix():
    r = np.arange(HG_CHUNK)[None, :]
    mats = [r <= np.arange(HG_CHUNK)[:, None]]
    for g in HG_LEVELS:
        for right in (True, False):
            t = _hgrn_half_rows(g, right)[:, None]
            bnd = (t // (2 * g)) * (2 * g) + g - 1
            mats.append((r > bnd) & (r <= t) if right else (r > t) & (r <= bnd))
    return np.concatenate(mats, axis=0).astype(np.float32)


def _hgrn_level_masks():
    half = HG_CHUNK // 2
    tc = np.arange(half)[:, None]
    sc = np.arange(2 * half)[None, :] % half
    return np.stack([(tc // g) == (sc // g) for g in HG_LEVELS]).astype(np.float32)


def _hgrn_consts():
    return (jnp.asarray(_hgrn_coef_matrix(), dtype=BF16), jnp.asarray(_hgrn_level_masks()))


def _hgrn_gmlp_kernel(layer, q_ref, f_ref, i_ref, g_ref, gu_ref, gv_ref, lbl_ref, ng_ref,
                      coef_ref, lmask_ref, gm_ng_ref, gm_ws_ref, gm_bias_ref, o_ref, gm_o_ref):
    s_len = q_ref.shape[0]
    gm_w_cat = _gmlp_weights(gm_ws_ref)
    c = HG_CHUNK
    half = c // 2
    logits = lbl_ref[...]
    e = jnp.exp(logits - jnp.max(logits, axis=0, keepdims=True))
    p = e / jnp.sum(e, axis=0, keepdims=True)
    lb = jnp.zeros((1, MIX_W), F32)
    for l in range(1, layer + 1):
        lb = lb + p[l:l + 1, :]

    same_head = (_head_of_lane((MIX_W, MIX_W), 0) == _head_of_lane((MIX_W, MIX_W), 1))
    head_ones = jnp.where(same_head, 1.0, 0.0).astype(BF16)
    head_rows = [jnp.where(_head_of_lane((1, MIX_W), 1) == h, 1.0, 0.0).astype(BF16)
                 for h in range(N_HEADS)]
    group_shape = (c // SUBLANES, SUBLANES, MIX_W)
    row_in_group = lax.broadcasted_iota(jnp.int32, group_shape, 1) % HG_BASE

    def body(j, st):
        r0 = pl.multiple_of(j * c, c)
        fz = f_ref[pl.ds(r0, c), :]
        qf = _silu(q_ref[pl.ds(r0, c), :])
        sg = _sigmoid(fz)
        f = lb + (1.0 - lb) * sg
        lf = jnp.log2(jnp.maximum(f, LOG_FLOOR))
        kf = (1.0 - lb) * (1.0 - sg)
        v = i_ref[pl.ds(r0, c), :]
        vb = v.astype(BF16)

        lf_hi, lf_lo = _split2(lf)
        pc = _dot(coef_ref[0:c, :], lf_hi) + _dot(coef_ref[0:c, :], lf_lo)
        sums = _dot(coef_ref[c:, :], lf_hi)
        ptot = pc[c - 1:c, :]

        qd = (qf * jnp.exp2(pc)).astype(BF16)
        terms = [_dot_nt(qd, st.astype(BF16))]
        kd = (kf * jnp.exp2(ptot - pc)).astype(BF16)
        st_next = st * jnp.exp2(ptot) + jnp.where(same_head, _dot_tn(vb, kd), 0.0)

        p3 = pc.reshape(group_shape)
        q3, k3, v3 = (x.reshape(group_shape) for x in (qf, kf, v))

        def group_term(d):
            if d == 0:
                prod, vd = q3 * k3, v3
            else:
                e = jnp.exp2(jnp.minimum(p3 - pltpu.roll(p3, d, 1), 0.0))
                prod = jnp.where(row_in_group >= d, q3 * pltpu.roll(k3, d, 1) * e, 0.0)
                vd = pltpu.roll(v3, d, 1)
            w = _dot(prod.reshape(c, MIX_W).astype(BF16), head_ones)
            return w * vd.reshape(c, MIX_W)

        def level_term(li, g):
            n_blocks = half // g

            def half_rows(x, off):
                return jnp.concatenate(
                    [x[2 * g * b + off:2 * g * b + off + g] for b in range(n_blocks)], axis=0)

            cq = sums[li * c:li * c + half]
            ck = sums[li * c + half:(li + 1) * c]
            qt = (half_rows(qf, g) * jnp.exp2(cq)).astype(BF16)
            kt = (half_rows(kf, 0) * jnp.exp2(ck)).astype(BF16)
            vt = half_rows(v, 0).astype(BF16)
            out = jnp.zeros((half, MIX_W), F32)
            for h in range(0, N_HEADS, 2):
                kw = jnp.concatenate([kt * head_rows[h], kt * head_rows[h + 1]], axis=0)
                vw = jnp.concatenate([vt * head_rows[h], vt * head_rows[h + 1]], axis=0)
                sc = _dot_nt(qt, kw) * lmask_ref[li]
                out = out + _dot(sc.astype(BF16), vw)
            zero = jnp.zeros((g, MIX_W), F32)
            return jnp.concatenate(
                [piece for b in range(n_blocks) for piece in (zero, out[b * g:(b + 1) * g])], axis=0)

        gm_blocks = list(range(0, c, GM_BLOCK))
        for i in range(max(HG_BASE, len(HG_LEVELS))):
            if i < HG_BASE:
                terms.append(group_term(i))
            if i < len(HG_LEVELS):
                terms.append(level_term(i, HG_LEVELS[i]))
            if i % 2 == 1 and gm_blocks:
                _gmlp_block(r0 + gm_blocks.pop(0), gu_ref, gv_ref, gm_ng_ref, gm_bias_ref,
                            gm_o_ref, gm_w_cat, head_rows)
        while len(terms) > 1:
            terms = [a + b for a, b in zip(terms[0::2], terms[1::2])] + terms[len(terms) & ~1:]
        o = terms[0]

        ms = _dot((o * o).astype(BF16), head_ones) * (1.0 / HEAD_DIM)
        o = o * lax.rsqrt(ms + EPS) * ng_ref[...]
        o = o * _silu(g_ref[pl.ds(r0, c), :])
        o_ref[pl.ds(r0, c), :] = o.astype(o_ref.dtype)
        return st_next

    lax.fori_loop(0, s_len // c, body, jnp.zeros((MIX_W, MIX_W), F32), unroll=8)


def _hgrn2_gmlp(z3, layer, lb_logits, hg_norm, coef, level_masks, gm_norm, gm_ws, gm_bias):
    b, s, _ = z3.shape
    col_spec = lambda c: pl.BlockSpec((None, s, MIX_W), lambda i, c=c: (i, 0, c))
    out_spec = pl.BlockSpec((None, s, MIX_W), lambda i: (i, 0, 0))
    out_shape = jax.ShapeDtypeStruct((b, s, MIX_W), BF16)
    return pl.pallas_call(
        functools.partial(_hgrn_gmlp_kernel, layer),
        grid=(b,),
        in_specs=[col_spec(0), col_spec(1), col_spec(2), col_spec(3), col_spec(4), col_spec(5),
                  pl.BlockSpec((DEPTH, MIX_W), lambda i: (0, 0)),
                  _layer_spec((1, MIX_W), layer),
                  pl.BlockSpec(coef.shape, lambda i: (0, 0)),
                  pl.BlockSpec(level_masks.shape, lambda i: (0, 0, 0)),
                  _layer_spec((1, MIX_W), layer),
                  _layer_spec((N_HEADS, GM_BLOCK, GM_BLOCK), layer),
                  _layer_spec((GM_BLOCK, MIX_W), layer)],
        out_specs=[out_spec, out_spec],
        out_shape=[out_shape, out_shape],
        compiler_params=_compiler_params(1),
        name="hgrn2_gmlp",
    )(z3, z3, z3, z3, z3, z3, lb_logits, hg_norm, coef, level_masks, gm_norm, gm_ws, gm_bias)


def _gmlp_weights(ws_ref):
    prow = lax.broadcasted_iota(jnp.int32, (GM_BLOCK, GM_BLOCK), 0)
    pcol = lax.broadcasted_iota(jnp.int32, (GM_BLOCK, GM_BLOCK), 1)
    return jnp.concatenate(
        [jnp.where(pcol <= prow, ws_ref[g], 0.0).astype(BF16) for g in range(N_HEADS)], axis=1)


def _gmlp_block(r0, u_ref, v_ref, ng_ref, bias_ref, o_ref, w_cat, group_rows):
    rows = pl.ds(r0, GM_BLOCK)
    vn = _rms(_gelu(v_ref[rows, :]), ng_ref[...]).astype(BF16)
    vn_cat = jnp.concatenate([vn * group_rows[g] for g in range(N_HEADS)], axis=0)
    mixed = bias_ref[...] + _dot(w_cat, vn_cat)
    o_ref[rows, :] = (_gelu(u_ref[rows, :]) * mixed).astype(o_ref.dtype)


LRU_ROW_CONV_B, LRU_ROW_BA, LRU_ROW_BX, LRU_ROW_LAM = 4, 5, 6, 7


def _lru_kernel(x_ref, gate_ref, vec_ref, w_ref, o_ref, a_ref, b_ref):
    s_len = x_ref.shape[0]
    vrow = lambda k: vec_ref[k:k + 1, :]
    x = x_ref[...]
    t_head = lax.broadcasted_iota(jnp.int32, (SUBLANES, MIX_W), 0)

    def first_tile(fn, v):
        return jnp.concatenate([fn(v[0:SUBLANES]), v[SUBLANES:]], axis=0)

    xc = vrow(LRU_ROW_CONV_B) + x * vrow(CONV_W - 1)
    for d in range(1, CONV_W):
        xs = first_tile(lambda v, d=d: jnp.where(t_head >= d, v, 0.0), pltpu.roll(x, d, 0))
        xc = xc + xs * vrow(CONV_W - 1 - d)
    xcb = xc.astype(BF16)
    r = _sigmoid(_dot(xcb, w_ref[0]) + vrow(LRU_ROW_BA))
    ig = _sigmoid(_dot(xcb, w_ref[1]) + vrow(LRU_ROW_BX))
    nl = -vrow(LRU_ROW_LAM)
    softplus = jnp.maximum(nl, 0.0) + jnp.log(1.0 + jnp.exp(-jnp.abs(nl)))
    a_full = jnp.exp(r * (-LRU_C * softplus))
    mult = jnp.sqrt(jnp.maximum(1.0 - a_full * a_full, 0.0))
    mult = first_tile(lambda v: jnp.where(t_head == 0, 1.0, v), mult)
    tiles = (s_len // SUBLANES, SUBLANES, MIX_W)
    a = a_full.reshape(tiles)
    b = (mult * (ig * xc)).reshape(tiles)
    row_in_tile = lax.broadcasted_iota(jnp.int32, tiles, 1)
    d = 1
    while d < SUBLANES:
        ok = row_in_tile >= d
        a_prev = jnp.where(ok, pltpu.roll(a, d, 1), 1.0)
        b_prev = jnp.where(ok, pltpu.roll(b, d, 1), 0.0)
        b = a * b_prev + b
        a = a * a_prev
        d *= 2
    a_ref[...] = a.reshape(s_len, MIX_W)
    b_ref[...] = b.reshape(s_len, MIX_W)
    carry = jnp.zeros((1, MIX_W), F32)
    for r0 in range(0, s_len, SUBLANES):
        rows = pl.ds(r0, SUBLANES)
        h = b_ref[rows, :] + a_ref[rows, :] * carry
        b_ref[rows, :] = h
        carry = h[SUBLANES - 1:SUBLANES, :]
    o_ref[...] = (b_ref[...] * _gelu(gate_ref[...])).astype(o_ref.dtype)


def _block_diag(w):
    eye = jnp.eye(N_HEADS, dtype=w.dtype)
    bd = jnp.einsum('...hij,hg->...higj', w, eye)
    return bd.reshape(w.shape[:-3] + (MIX_W, MIX_W))


def _rg_lru(z3, layer, vec, w_bd):
    b, s, _ = z3.shape
    col_spec = lambda c: pl.BlockSpec((None, s, MIX_W), lambda i, c=c: (i, 0, c))
    return pl.pallas_call(
        _lru_kernel,
        grid=(b,),
        in_specs=[col_spec(6), col_spec(7),
                  _layer_spec((8, MIX_W), layer),
                  _layer_spec((2, MIX_W, MIX_W), layer)],
        out_specs=pl.BlockSpec((None, s, MIX_W), lambda i: (i, 0, 0)),
        out_shape=jax.ShapeDtypeStruct((b, s, MIX_W), BF16),
        scratch_shapes=[pltpu.VMEM((s, MIX_W), F32), pltpu.VMEM((s, MIX_W), F32)],
        compiler_params=_compiler_params(1),
        name="rg_lru",
    )(z3, z3, vec, w_bd)


def _mix_out_kernel(cast_cols, x_ref, oa_ref, ob_ref, oc_ref, od_ref, gpre_ref, gpost_ref,
                    wg_ref, wb_ref, wo_ref, *refs):
    n_cast = len(cast_cols)
    out_ref = refs[n_cast]
    _cast_slabs(cast_cols, refs[:n_cast], refs[n_cast + 1:])
    for rows in _sub_tiles(x_ref.shape[0]):
        x = x_ref[rows, :]
        h = _rms(x, gpre_ref[...]).astype(BF16)
        merged = jnp.zeros(x.shape, F32)
        for n, o_ref in enumerate((oa_ref, ob_ref, oc_ref, od_ref)):
            gate = _dot(h, wg_ref[:, n * D_MODEL:(n + 1) * D_MODEL])
            proj = _dot(o_ref[rows, :], wb_ref[n])
            merged = merged + _sigmoid(gate) * proj
        y = _dot(merged.astype(BF16), wo_ref[...])
        out_ref[rows, :] = x + _rms(y, gpost_ref[...])


def _mix_out(x2, outs, layer, g_pre, g_post, w_gate, w_branch, w_out, cast_jobs=()):
    t = x2.shape[0]
    n_steps = t // TOKEN_TILE
    tok = lambda w: pl.BlockSpec((TOKEN_TILE, w), lambda i: (i, 0))
    cast_in, cast_out, cast_shapes = _cast_specs(cast_jobs, n_steps)
    return pl.pallas_call(
        functools.partial(_mix_out_kernel, tuple(job[2] for job in cast_jobs)),
        grid=(n_steps,),
        in_specs=[tok(D_MODEL), tok(MIX_W), tok(MIX_W), tok(MIX_W), tok(MIX_W),
                  _layer_spec((1, D_MODEL), layer), _layer_spec((1, D_MODEL), layer),
                  _resident_spec(w_gate.shape), _resident_spec(w_branch.shape),
                  _resident_spec(w_out.shape)] + cast_in,
        out_specs=[tok(D_MODEL)] + cast_out,
        out_shape=[jax.ShapeDtypeStruct((t, D_MODEL), F32)] + cast_shapes,
        compiler_params=_compiler_params(1),
        name="mix_out",
    )(x2, *outs, g_pre, g_post, w_gate, w_branch, w_out, *(job[0] for job in cast_jobs))


def _ffn_kernel(cast_cols, x_ref, gpre_ref, gpost_ref, w1_ref, w2_ref, *refs):
    n_cast = len(cast_cols)
    out_ref = refs[n_cast]
    _cast_slabs(cast_cols, refs[:n_cast], refs[n_cast + 1:])
    for rows in _sub_tiles(x_ref.shape[0]):
        x = x_ref[rows, :]
        h = _rms(x, gpre_ref[...]).astype(BF16)
        y = jnp.zeros(x.shape, F32)
        for lo, hi in FFN_HIDDEN_CHUNKS:
            gt = _dot(h, w1_ref[:, lo:hi])
            up = _dot(h, w1_ref[:, FFN_HIDDEN + lo:FFN_HIDDEN + hi])
            act = (_silu(gt) * up).astype(BF16)
            y = y + _dot(act, w2_ref[lo:hi, :])
        out_ref[rows, :] = x + _rms(y, gpost_ref[...])


def _ffn(x2, layer, g_pre, g_post, w1, w2, cast_jobs=()):
    t = x2.shape[0]
    n_steps = t // FFN_TOKEN_TILE
    tok = pl.BlockSpec((FFN_TOKEN_TILE, D_MODEL), lambda i: (i, 0))
    cast_in, cast_out, cast_shapes = _cast_specs(cast_jobs, n_steps)
    return pl.pallas_call(
        functools.partial(_ffn_kernel, tuple(job[2] for job in cast_jobs)),
        grid=(n_steps,),
        in_specs=[tok, _layer_spec((1, D_MODEL), layer), _layer_spec((1, D_MODEL), layer),
                  _resident_spec(w1.shape), _resident_spec(w2.shape)] + cast_in,
        out_specs=[tok] + cast_out,
        out_shape=[jax.ShapeDtypeStruct((t, D_MODEL), F32)] + cast_shapes,
        compiler_params=_compiler_params(1),
        name="ffn",
    )(x2, g_pre, g_post, w1, w2, *(job[0] for job in cast_jobs))


def kernel(x, norm_mix_pre, norm_mix_post, norm_ffn_pre, norm_ffn_post, w_in, attn_rel_bias, hgrn_lb_logits, hgrn_norm_g, gmlp_norm_g, gmlp_ws, gmlp_bs, lru_conv_w, lru_conv_b, lru_wa, lru_ba, lru_wx, lru_bx, lru_lambda, w_branch, w_out, w_ffn_in, w_ffn_out):
    b, s, d = x.shape
    x2 = x.reshape(b * s, d)
    w_mix, w_gate = _split_cast_w_in(w_in, 0)
    w_br = w_branch[0].astype(BF16)
    w_o = w_out[0].astype(BF16)
    w1 = w_ffn_in[0].astype(BF16)
    w2 = w_ffn_out[0].astype(BF16)
    w_branch_rows = w_branch.reshape(DEPTH, N_BRANCH * MIX_W, D_MODEL)
    all_cols = lambda stack: ((0, stack.shape[2]),)
    row = lambda p: p.reshape(DEPTH, 1, p.shape[-1])
    g_mix_pre, g_mix_post = row(norm_mix_pre), row(norm_mix_post)
    g_ffn_pre, g_ffn_post = row(norm_ffn_pre), row(norm_ffn_post)
    bias_rows = _attn_bias_rows(attn_rel_bias)
    hg_consts = _hgrn_consts()
    hg_norm = row(hgrn_norm_g)
    gm_norm = row(gmlp_norm_g)
    gm_bias = jnp.repeat(jnp.swapaxes(gmlp_bs, 1, 2), HEAD_DIM, axis=2)
    lru_vec = jnp.concatenate(
        [lru_conv_w, lru_conv_b[:, None], lru_ba[:, None], lru_bx[:, None], lru_lambda[:, None]],
        axis=1)
    lru_w = _block_diag(jnp.stack([lru_wa, lru_wx], axis=1)).astype(BF16)
    for l in range(DEPTH):
        nxt = l + 1
        stage = nxt < DEPTH
        jobs = lambda *stacks: tuple((st, nxt, all_cols(st)) for st in stacks) if stage else ()
        z_att, z_rest, *staged_a = _in_proj(x2, l, g_mix_pre, w_mix, jobs(w_branch_rows, w_out))
        z3 = z_rest.reshape(b, s, N_MIX_COLS - N_ATT_COLS)
        o_a = _attention(z_att.reshape(b, s, N_ATT_COLS), l, bias_rows)
        o_b, o_c = _hgrn2_gmlp(z3, l, hgrn_lb_logits, hg_norm, *hg_consts, gm_norm, gmlp_ws, gm_bias)
        o_d = _rg_lru(z3, l, lru_vec, lru_w)
        outs = [o.reshape(b * s, MIX_W) for o in (o_a, o_b, o_c, o_d)]
        w_in_job = ((w_in, nxt, ((0, N_MIX_COLS), (N_MIX_COLS, w_in.shape[2]))),) if stage else ()
        x2, *staged_b = _mix_out(x2, outs, l, g_mix_pre, g_mix_post, w_gate, w_br, w_o, w_in_job)
        x2, *staged_c = _ffn(x2, l, g_ffn_pre, g_ffn_post, w1, w2, jobs(w_ffn_in, w_ffn_out))
        if stage:
            w_br, w_o = staged_a[0].reshape(N_BRANCH, MIX_W, D_MODEL), staged_a[1]
            w_mix, w_gate = staged_b
            w1, w2 = staged_c
    return x2.reshape(b, s, d)
```

```python
import functools

import numpy as np
import jax
import jax.numpy as jnp
from jax import lax
from jax.experimental import pallas as pl
from jax.experimental.pallas import tpu as pltpu

D_MODEL = 1024
DEPTH = 4
CHUNK = 64
EPS = 1e-6
N_HEADS = 4
HEAD_DIM = 64
MIX_W = N_HEADS * HEAD_DIM
N_BRANCH = 4
ATT_LEFT_CHUNKS = 8
REL_MAX = 256
NEG_BIG = -1e30
LOG_FLOOR = 1e-30
GM_BLOCK = 128
CONV_W = 4
LRU_C = 8.0
FFN_HIDDEN = 2816
N_MIX_COLS = 11 * MIX_W
N_ATT_COLS = 3 * MIX_W
LOG2_E = 1.4426950408889634
ATT_Q_SCALE = HEAD_DIM ** -0.5 * LOG2_E

F32 = jnp.float32
BF16 = jnp.bfloat16

VMEM_LIMIT_BYTES = 56 * 1024 * 1024
TOKEN_TILE = 1024
FFN_TOKEN_TILE = 1024
SUB_TILE = 256
ATT_QB = 256
ATT_PAD = ATT_LEFT_CHUNKS * CHUNK
ATT_KB = ATT_PAD + ATT_QB
ATT_ROLL_W = 1024
MXU_TILE = 256
FFN_HIDDEN_CHUNKS = ((0, 6 * MXU_TILE), (6 * MXU_TILE, FFN_HIDDEN))
HG_CHUNK = 256
SUBLANES = 8
HG_BASE = 4
HG_LEVELS = (4, 8, 16, 32, 64, 128)


def _compiler_params(n_grid):
    return pltpu.CompilerParams(
        dimension_semantics=("arbitrary",) * n_grid,
        vmem_limit_bytes=VMEM_LIMIT_BYTES)


def _rms(x, g):
    return x * lax.rsqrt(jnp.mean(x * x, axis=-1, keepdims=True) + EPS) * g


def _gelu(x):
    c = 0.7978845608028654
    half_x = 0.5 * x
    return half_x + half_x * jnp.tanh(x * (c + (c * 0.044715) * (x * x)))


def _sigmoid(x):
    return 1.0 / (1.0 + jnp.exp(-x))


def _silu(x):
    return x * _sigmoid(x)


def _dot(a, b):
    return jnp.dot(a, b, preferred_element_type=F32)


def _dot_nt(a, b):
    return lax.dot_general(a, b, (((1,), (1,)), ((), ())), preferred_element_type=F32)


def _dot_tn(a, b):
    return lax.dot_general(a, b, (((0,), (0,)), ((), ())), preferred_element_type=F32)


def _split2(x):
    hi = x.astype(BF16)
    lo = (x - hi.astype(F32)).astype(BF16)
    return hi, lo


def _head_of_lane(shape, dim):
    return lax.broadcasted_iota(jnp.int32, shape, dim) // HEAD_DIM


def _resident_spec(shape):
    zeros = (0,) * len(shape)
    return pl.BlockSpec(tuple(shape), lambda i: zeros, pipeline_mode=pl.Buffered(1))


def _cast_specs(jobs, n_steps):
    in_specs, out_specs, out_shapes = [], [], []
    for stack, layer, col_ranges in jobs:
        _, n_rows, n_cols = stack.shape
        slab = n_rows // n_steps
        in_specs.append(pl.BlockSpec((None, slab, n_cols), lambda i, layer=layer: (layer, i, 0)))
        for lo, hi in col_ranges:
            out_specs.append(pl.BlockSpec((slab, hi - lo), lambda i: (i, 0)))
            out_shapes.append(jax.ShapeDtypeStruct((n_rows, hi - lo), BF16))
    return in_specs, out_specs, out_shapes


def _cast_slabs(col_ranges_per_job, src_refs, dst_refs):
    dst_refs = iter(dst_refs)
    for col_ranges, src_ref in zip(col_ranges_per_job, src_refs):
        for lo, hi in col_ranges:
            next(dst_refs)[...] = src_ref[:, lo:hi].astype(BF16)


def _layer_spec(shape, layer):
    zeros = (0,) * len(shape)
    return pl.BlockSpec((None,) + tuple(shape), lambda i: (layer,) + zeros,
                        pipeline_mode=pl.Buffered(1))


def _sub_tiles(n_rows):
    return tuple(pl.ds(r, SUB_TILE) for r in range(0, n_rows, SUB_TILE))


def _in_proj_kernel(cast_cols, x_ref, g_ref, w_ref, *refs):
    n_cast = len(cast_cols)
    za_ref, zr_ref = refs[n_cast:n_cast + 2]
    _cast_slabs(cast_cols, refs[:n_cast], refs[n_cast + 2:])
    for rows in _sub_tiles(x_ref.shape[0]):
        h = _rms(x_ref[rows, :], g_ref[...]).astype(BF16)
        z = _dot(h, w_ref[...])
        za_ref[rows, 0:MIX_W] = (z[:, 0:MIX_W] * ATT_Q_SCALE).astype(BF16)
        za_ref[rows, MIX_W:N_ATT_COLS] = z[:, MIX_W:N_ATT_COLS].astype(BF16)
        zr_ref[rows, :] = z[:, N_ATT_COLS:]


def _in_proj(x2, layer, g, w_mix, cast_jobs=()):
    t = x2.shape[0]
    n_rest = N_MIX_COLS - N_ATT_COLS
    n_steps = t // TOKEN_TILE
    cast_in, cast_out, cast_shapes = _cast_specs(cast_jobs, n_steps)
    return pl.pallas_call(
        functools.partial(_in_proj_kernel, tuple(job[2] for job in cast_jobs)),
        grid=(n_steps,),
        in_specs=[
            pl.BlockSpec((TOKEN_TILE, D_MODEL), lambda i: (i, 0)),
            _layer_spec((1, D_MODEL), layer),
            _resident_spec(w_mix.shape),
        ] + cast_in,
        out_specs=[pl.BlockSpec((TOKEN_TILE, N_ATT_COLS), lambda i: (i, 0)),
                   pl.BlockSpec((TOKEN_TILE, n_rest), lambda i: (i, 0))] + cast_out,
        out_shape=[jax.ShapeDtypeStruct((t, N_ATT_COLS), BF16),
                   jax.ShapeDtypeStruct((t, n_rest), F32)] + cast_shapes,
        compiler_params=_compiler_params(1),
        name="in_proj",
    )(x2, g, w_mix, *(job[0] for job in cast_jobs))


def _attn_kernel(q_ref, k_ref, v_ref, brow_ref, o_ref, bias_scr):
    s_len = q_ref.shape[0]

    @pl.when(pl.program_id(0) == 0)
    def _():
        qi = lax.broadcasted_iota(jnp.int32, (ATT_QB, ATT_KB), 0) // CHUNK
        kc = lax.broadcasted_iota(jnp.int32, (ATT_QB, ATT_KB), 1) // CHUNK
        band = (kc >= qi) & (kc <= qi + ATT_LEFT_CHUNKS)
        for h in range(N_HEADS):
            rows = jnp.broadcast_to(brow_ref[h:h + 1, :], (ATT_QB, ATT_ROLL_W))
            tile = pltpu.roll(rows, ATT_ROLL_W - ATT_QB, 1, stride=1, stride_axis=0)
            bias_scr[h] = jnp.where(band, tile[:, :ATT_KB], NEG_BIG)

    lane_head = _head_of_lane((ATT_QB, MIX_W), 1)
    head_rows = [jnp.where(_head_of_lane((1, MIX_W), 1) == h, 1.0, 0.0).astype(BF16)
                 for h in range(N_HEADS)]

    def block(r0, k0, n_keys):
        q = q_ref[pl.ds(r0, ATT_QB), :]
        kb = k_ref[pl.ds(k0, n_keys), :]
        vb = v_ref[pl.ds(k0, n_keys), :]
        acc = jnp.zeros((ATT_QB, MIX_W), F32)
        for h in range(N_HEADS):
            s = _dot_nt(q * head_rows[h], kb) + bias_scr[h, :, pl.ds(ATT_KB - n_keys, n_keys)]
            m = jnp.max(s, axis=-1, keepdims=True)
            p = jnp.exp2(s - m)
            l = jnp.sum(p, axis=-1, keepdims=True)
            o = _dot(p.astype(BF16), vb)
            acc = acc + jnp.where(lane_head == h, o * (1.0 / l), 0.0)
        o_ref[pl.ds(r0, ATT_QB), :] = acc.astype(o_ref.dtype)

    n_head_blocks = ATT_PAD // ATT_QB
    for j in range(n_head_blocks):
        block(j * ATT_QB, 0, (j + 1) * ATT_QB)

    def body(j, carry):
        r0 = pl.multiple_of(j * ATT_QB, ATT_QB)
        block(r0, r0 - ATT_PAD, ATT_KB)
        return carry

    lax.fori_loop(n_head_blocks, s_len // ATT_QB, body, 0, unroll=6)


def _attn_bias_rows(rel_bias):
    dist = ATT_PAD + ATT_QB - np.arange(ATT_ROLL_W)
    idx = np.clip(dist, -(CHUNK - 1), REL_MAX) + (CHUNK - 1)
    return rel_bias.astype(F32)[:, :, idx] * LOG2_E


def _attention(za3, layer, bias_rows):
    b, s, _ = za3.shape
    col_spec = lambda c: pl.BlockSpec((None, s, MIX_W), lambda i, c=c: (i, 0, c))
    return pl.pallas_call(
        _attn_kernel,
        grid=(b,),
        in_specs=[col_spec(0), col_spec(1), col_spec(2),
                  _layer_spec((N_HEADS, ATT_ROLL_W), layer)],
        out_specs=pl.BlockSpec((None, s, MIX_W), lambda i: (i, 0, 0)),
        out_shape=jax.ShapeDtypeStruct((b, s, MIX_W), BF16),
        scratch_shapes=[pltpu.VMEM((N_HEADS, ATT_QB, ATT_KB), F32)],
        compiler_params=_compiler_params(1),
        name="attention",
    )(za3, za3, za3, bias_rows)


def _hgrn_half_rows(g, right):
    off = g if right else 0
    return np.concatenate([np.arange(2 * g * b + off, 2 * g * b + off + g)
                           for b in range(HG_CHUNK // (2 * g))])


def _hgrn_coef_matrix():
    r = np.arange(HG_CHUNK)[None, :]
    mats = [r <= np.arange(HG_CHUNK)[:, None]]
    for g in HG_LEVELS:
        for right in (True, False):
            t = _hgrn_half_rows(g, right)[:, None]
            bnd = (t // (2 * g)) * (2 * g) + g - 1
            mats.append((r > bnd) & (r <= t) if right else (r > t) & (r <= bnd))
    return np.concatenate(mats, axis=0).astype(np.float32)


def _hgrn_level_masks():
    half = HG_CHUNK // 2
    tc = np.arange(half)[:, None]
    sc = np.arange(2 * half)[None, :] % half
    return np.stack([(tc // g) == (sc // g) for g in HG_LEVELS]).astype(np.float32)


def _hgrn_consts():
    return (jnp.asarray(_hgrn_coef_matrix(), dtype=BF16), jnp.asarray(_hgrn_level_masks()))


def _hgrn_gmlp_kernel(layer, q_ref, f_ref, i_ref, g_ref, gu_ref, gv_ref, lbl_ref, ng_ref,
                      coef_ref, lmask_ref, gm_ng_ref, gm_ws_ref, gm_bias_ref, o_ref, gm_o_ref):
    s_len = q_ref.shape[0]
    gm_w_cat = _gmlp_weights(gm_ws_ref)
    c = HG_CHUNK
    half = c // 2
    logits = lbl_ref[...]
    e = jnp.exp(logits - jnp.max(logits, axis=0, keepdims=True))
    p = e / jnp.sum(e, axis=0, keepdims=True)
    lb = jnp.zeros((1, MIX_W), F32)
    for l in range(1, layer + 1):
        lb = lb + p[l:l + 1, :]

    same_head = (_head_of_lane((MIX_W, MIX_W), 0) == _head_of_lane((MIX_W, MIX_W), 1))
    head_ones = jnp.where(same_head, 1.0, 0.0).astype(BF16)
    head_rows = [jnp.where(_head_of_lane((1, MIX_W), 1) == h, 1.0, 0.0).astype(BF16)
                 for h in range(N_HEADS)]
    group_shape = (c // SUBLANES, SUBLANES, MIX_W)
    row_in_group = lax.broadcasted_iota(jnp.int32, group_shape, 1) % HG_BASE

    def body(j, st):
        r0 = pl.multiple_of(j * c, c)
        fz = f_ref[pl.ds(r0, c), :]
        qf = _silu(q_ref[pl.ds(r0, c), :])
        sg = _sigmoid(fz)
        f = lb + (1.0 - lb) * sg
        lf = jnp.log2(jnp.maximum(f, LOG_FLOOR))
        kf = (1.0 - lb) * (1.0 - sg)
        v = i_ref[pl.ds(r0, c), :]
        vb = v.astype(BF16)

        lf_hi, lf_lo = _split2(lf)
        pc = _dot(coef_ref[0:c, :], lf_hi) + _dot(coef_ref[0:c, :], lf_lo)
        sums = _dot(coef_ref[c:, :], lf_hi)
        ptot = pc[c - 1:c, :]

        qd = (qf * jnp.exp2(pc)).astype(BF16)
        terms = [_dot_nt(qd, st.astype(BF16))]
        kd = (kf * jnp.exp2(ptot - pc)).astype(BF16)
        st_next = st * jnp.exp2(ptot) + jnp.where(same_head, _dot_tn(vb, kd), 0.0)

        p3 = pc.reshape(group_shape)
        q3, k3, v3 = (x.reshape(group_shape) for x in (qf, kf, v))

        def group_term(d):
            if d == 0:
                prod, vd = q3 * k3, v3
            else:
                e = jnp.exp2(jnp.minimum(p3 - pltpu.roll(p3, d, 1), 0.0))
                prod = jnp.where(row_in_group >= d, q3 * pltpu.roll(k3, d, 1) * e, 0.0)
                vd = pltpu.roll(v3, d, 1)
            w = _dot(prod.reshape(c, MIX_W).astype(BF16), head_ones)
            return w * vd.reshape(c, MIX_W)

        def level_term(li, g):
            n_blocks = half // g

            def half_rows(x, off):
                return jnp.concatenate(
                    [x[2 * g * b + off:2 * g * b + off + g] for b in range(n_blocks)], axis=0)

            cq = sums[li * c:li * c + half]
            ck = sums[li * c + half:(li + 1) * c]
            qt = (half_rows(qf, g) * jnp.exp2(cq)).astype(BF16)
            kt = (half_rows(kf, 0) * jnp.exp2(ck)).astype(BF16)
            vt = half_rows(v, 0).astype(BF16)
            out = jnp.zeros((half, MIX_W), F32)
            for h in range(0, N_HEADS, 2):
                kw = jnp.concatenate([kt * head_rows[h], kt * head_rows[h + 1]], axis=0)
                vw = jnp.concatenate([vt * head_rows[h], vt * head_rows[h + 1]], axis=0)
                sc = _dot_nt(qt, kw) * lmask_ref[li]
                out = out + _dot(sc.astype(BF16), vw)
            zero = jnp.zeros((g, MIX_W), F32)
            return jnp.concatenate(
                [piece for b in range(n_blocks) for piece in (zero, out[b * g:(b + 1) * g])], axis=0)

        gm_blocks = list(range(0, c, GM_BLOCK))
        for i in range(max(HG_BASE, len(HG_LEVELS))):
            if i < HG_BASE:
                terms.append(group_term(i))
            if i < len(HG_LEVELS):
                terms.append(level_term(i, HG_LEVELS[i]))
            if i % 2 == 1 and gm_blocks:
                _gmlp_block(r0 + gm_blocks.pop(0), gu_ref, gv_ref, gm_ng_ref, gm_bias_ref,
                            gm_o_ref, gm_w_cat, head_rows)
        while len(terms) > 1:
            terms = [a + b for a, b in zip(terms[0::2], terms[1::2])] + terms[len(terms) & ~1:]
        o = terms[0]

        ms = _dot((o * o).astype(BF16), head_ones) * (1.0 / HEAD_DIM)
        o = o * lax.rsqrt(ms + EPS) * ng_ref[...]
        o = o * _silu(g_ref[pl.ds(r0, c), :])
        o_ref[pl.ds(r0, c), :] = o.astype(o_ref.dtype)
        return st_next

    lax.fori_loop(0, s_len // c, body, jnp.zeros((MIX_W, MIX_W), F32), unroll=8)


def _hgrn2_gmlp(z3, layer, lb_logits, hg_norm, coef, level_masks, gm_norm, gm_ws, gm_bias):
    b, s, _ = z3.shape
    col_spec = lambda c: pl.BlockSpec((None, s, MIX_W), lambda i, c=c: (i, 0, c))
    out_spec = pl.BlockSpec((None, s, MIX_W), lambda i: (i, 0, 0))
    out_shape = jax.ShapeDtypeStruct((b, s, MIX_W), BF16)
    return pl.pallas_call(
        functools.partial(_hgrn_gmlp_kernel, layer),
        grid=(b,),
        in_specs=[col_spec(0), col_spec(1), col_spec(2), col_spec(3), col_spec(4), col_spec(5),
                  pl.BlockSpec((DEPTH, MIX_W), lambda i: (0, 0)),
                  _layer_spec((1, MIX_W), layer),
                  pl.BlockSpec(coef.shape, lambda i: (0, 0)),
                  pl.BlockSpec(level_masks.shape, lambda i: (0, 0, 0)),
                  _layer_spec((1, MIX_W), layer),
                  _layer_spec((N_HEADS, GM_BLOCK, GM_BLOCK), layer),
                  _layer_spec((GM_BLOCK, MIX_W), layer)],
        out_specs=[out_spec, out_spec],
        out_shape=[out_shape, out_shape],
        compiler_params=_compiler_params(1),
        name="hgrn2_gmlp",
    )(z3, z3, z3, z3, z3, z3, lb_logits, hg_norm, coef, level_masks, gm_norm, gm_ws, gm_bias)


def _gmlp_weights(ws_ref):
    prow = lax.broadcasted_iota(jnp.int32, (GM_BLOCK, GM_BLOCK), 0)
    pcol = lax.broadcasted_iota(jnp.int32, (GM_BLOCK, GM_BLOCK), 1)
    return jnp.concatenate(
        [jnp.where(pcol <= prow, ws_ref[g], 0.0).astype(BF16) for g in range(N_HEADS)], axis=1)


def _gmlp_block(r0, u_ref, v_ref, ng_ref, bias_ref, o_ref, w_cat, group_rows):
    rows = pl.ds(r0, GM_BLOCK)
    vn = _rms(_gelu(v_ref[rows, :]), ng_ref[...]).astype(BF16)
    vn_cat = jnp.concatenate([vn * group_rows[g] for g in range(N_HEADS)], axis=0)
    mixed = bias_ref[...] + _dot(w_cat, vn_cat)
    o_ref[rows, :] = (_gelu(u_ref[rows, :]) * mixed).astype(o_ref.dtype)


LRU_ROW_CONV_B, LRU_ROW_BA, LRU_ROW_BX, LRU_ROW_LAM = 4, 5, 6, 7


def _lru_kernel(x_ref, gate_ref, vec_ref, w_ref, o_ref, a_ref, b_ref):
    s_len = x_ref.shape[0]
    vrow = lambda k: vec_ref[k:k + 1, :]
    x = x_ref[...]
    t_head = lax.broadcasted_iota(jnp.int32, (SUBLANES, MIX_W), 0)

    def first_tile(fn, v):
        return jnp.concatenate([fn(v[0:SUBLANES]), v[SUBLANES:]], axis=0)

    xc = vrow(LRU_ROW_CONV_B) + x * vrow(CONV_W - 1)
    for d in range(1, CONV_W):
        xs = first_tile(lambda v, d=d: jnp.where(t_head >= d, v, 0.0), pltpu.roll(x, d, 0))
        xc = xc + xs * vrow(CONV_W - 1 - d)
    xcb = xc.astype(BF16)
    r = _sigmoid(_dot(xcb, w_ref[0]) + vrow(LRU_ROW_BA))
    ig = _sigmoid(_dot(xcb, w_ref[1]) + vrow(LRU_ROW_BX))
    nl = -vrow(LRU_ROW_LAM)
    softplus = jnp.maximum(nl, 0.0) + jnp.log(1.0 + jnp.exp(-jnp.abs(nl)))
    a_full = jnp.exp(r * (-LRU_C * softplus))
    mult = jnp.sqrt(jnp.maximum(1.0 - a_full * a_full, 0.0))
    mult = first_tile(lambda v: jnp.where(t_head == 0, 1.0, v), mult)
    tiles = (s_len // SUBLANES, SUBLANES, MIX_W)
    a = a_full.reshape(tiles)
    b = (mult * (ig * xc)).reshape(tiles)
    row_in_tile = lax.broadcasted_iota(jnp.int32, tiles, 1)
    d = 1
    while d < SUBLANES:
        ok = row_in_tile >= d
        a_prev = jnp.where(ok, pltpu.roll(a, d, 1), 1.0)
        b_prev = jnp.where(ok, pltpu.roll(b, d, 1), 0.0)
        b = a * b_prev + b
        a = a * a_prev
        d *= 2
    a_ref[...] = a.reshape(s_len, MIX_W)
    b_ref[...] = b.reshape(s_len, MIX_W)
    carry = jnp.zeros((1, MIX_W), F32)
    for r0 in range(0, s_len, SUBLANES):
        rows = pl.ds(r0, SUBLANES)
        h = b_ref[rows, :] + a_ref[rows, :] * carry
        b_ref[rows, :] = h
        carry = h[SUBLANES - 1:SUBLANES, :]
    o_ref[...] = (b_ref[...] * _gelu(gate_ref[...])).astype(o_ref.dtype)


def _block_diag(w):
    eye = jnp.eye(N_HEADS, dtype=w.dtype)
    bd = jnp.einsum('...hij,hg->...higj', w, eye)
    return bd.reshape(w.shape[:-3] + (MIX_W, MIX_W))


def _rg_lru(z3, layer, vec, w_bd):
    b, s, _ = z3.shape
    col_spec = lambda c: pl.BlockSpec((None, s, MIX_W), lambda i, c=c: (i, 0, c))
    return pl.pallas_call(
        _lru_kernel,
        grid=(b,),
        in_specs=[col_spec(6), col_spec(7),
                  _layer_spec((8, MIX_W), layer),
                  _layer_spec((2, MIX_W, MIX_W), layer)],
        out_specs=pl.BlockSpec((None, s, MIX_W), lambda i: (i, 0, 0)),
        out_shape=jax.ShapeDtypeStruct((b, s, MIX_W), BF16),
        scratch_shapes=[pltpu.VMEM((s, MIX_W), F32), pltpu.VMEM((s, MIX_W), F32)],
        compiler_params=_compiler_params(1),
        name="rg_lru",
    )(z3, z3, vec, w_bd)


def _mix_out_kernel(cast_cols, x_ref, oa_ref, ob_ref, oc_ref, od_ref, gpre_ref, gpost_ref,
                    wg_ref, wb_ref, wo_ref, *refs):
    n_cast = len(cast_cols)
    out_ref = refs[n_cast]
    _cast_slabs(cast_cols, refs[:n_cast], refs[n_cast + 1:])
    for rows in _sub_tiles(x_ref.shape[0]):
        x = x_ref[rows, :]
        h = _rms(x, gpre_ref[...]).astype(BF16)
        merged = jnp.zeros(x.shape, F32)
        for n, o_ref in enumerate((oa_ref, ob_ref, oc_ref, od_ref)):
            gate = _dot(h, wg_ref[:, n * D_MODEL:(n + 1) * D_MODEL])
            proj = _dot(o_ref[rows, :], wb_ref[n])
            merged = merged + _sigmoid(gate) * proj
        y = _dot(merged.astype(BF16), wo_ref[...])
        out_ref[rows, :] = x + _rms(y, gpost_ref[...])


def _mix_out(x2, outs, layer, g_pre, g_post, w_gate, w_branch, w_out, cast_jobs=()):
    t = x2.shape[0]
    n_steps = t // TOKEN_TILE
    tok = lambda w: pl.BlockSpec((TOKEN_TILE, w), lambda i: (i, 0))
    cast_in, cast_out, cast_shapes = _cast_specs(cast_jobs, n_steps)
    return pl.pallas_call(
        functools.partial(_mix_out_kernel, tuple(job[2] for job in cast_jobs)),
        grid=(n_steps,),
        in_specs=[tok(D_MODEL), tok(MIX_W), tok(MIX_W), tok(MIX_W), tok(MIX_W),
                  _layer_spec((1, D_MODEL), layer), _layer_spec((1, D_MODEL), layer),
                  _resident_spec(w_gate.shape), _resident_spec(w_branch.shape),
                  _resident_spec(w_out.shape)] + cast_in,
        out_specs=[tok(D_MODEL)] + cast_out,
        out_shape=[jax.ShapeDtypeStruct((t, D_MODEL), F32)] + cast_shapes,
        compiler_params=_compiler_params(1),
        name="mix_out",
    )(x2, *outs, g_pre, g_post, w_gate, w_branch, w_out, *(job[0] for job in cast_jobs))


def _ffn_kernel(cast_cols, x_ref, gpre_ref, gpost_ref, w1_ref, w2_ref, *refs):
    n_cast = len(cast_cols)
    out_ref = refs[n_cast]
    _cast_slabs(cast_cols, refs[:n_cast], refs[n_cast + 1:])
    for rows in _sub_tiles(x_ref.shape[0]):
        x = x_ref[rows, :]
        h = _rms(x, gpre_ref[...]).astype(BF16)
        y = jnp.zeros(x.shape, F32)
        for lo, hi in FFN_HIDDEN_CHUNKS:
            gt = _dot(h, w1_ref[:, lo:hi])
            up = _dot(h, w1_ref[:, FFN_HIDDEN + lo:FFN_HIDDEN + hi])
            act = (_silu(gt) * up).astype(BF16)
            y = y + _dot(act, w2_ref[lo:hi, :])
        out_ref[rows, :] = x + _rms(y, gpost_ref[...])


def _ffn(x2, layer, g_pre, g_post, w1, w2, cast_jobs=()):
    t = x2.shape[0]
    n_steps = t // FFN_TOKEN_TILE
    tok = pl.BlockSpec((FFN_TOKEN_TILE, D_MODEL), lambda i: (i, 0))
    cast_in, cast_out, cast_shapes = _cast_specs(cast_jobs, n_steps)
    return pl.pallas_call(
        functools.partial(_ffn_kernel, tuple(job[2] for job in cast_jobs)),
        grid=(n_steps,),
        in_specs=[tok, _layer_spec((1, D_MODEL), layer), _layer_spec((1, D_MODEL), layer),
                  _resident_spec(w1.shape), _resident_spec(w2.shape)] + cast_in,
        out_specs=[tok] + cast_out,
        out_shape=[jax.ShapeDtypeStruct((t, D_MODEL), F32)] + cast_shapes,
        compiler_params=_compiler_params(1),
        name="ffn",
    )(x2, g_pre, g_post, w1, w2, *(job[0] for job in cast_jobs))


def kernel(x, norm_mix_pre, norm_mix_post, norm_ffn_pre, norm_ffn_post, w_in, attn_rel_bias, hgrn_lb_logits, hgrn_norm_g, gmlp_norm_g, gmlp_ws, gmlp_bs, lru_conv_w, lru_conv_b, lru_wa, lru_ba, lru_wx, lru_bx, lru_lambda, w_branch, w_out, w_ffn_in, w_ffn_out):
    b, s, d = x.shape
    x2 = x.reshape(b * s, d)
    w_mix = w_in[0, :, :N_MIX_COLS].astype(BF16)
    w_branch_rows = w_branch.reshape(DEPTH, N_BRANCH * MIX_W, D_MODEL)
    whole = lambda stack, layer: (stack, layer, ((0, stack.shape[2]),))
    row = lambda p: p.reshape(DEPTH, 1, p.shape[-1])
    g_mix_pre, g_mix_post = row(norm_mix_pre), row(norm_mix_post)
    g_ffn_pre, g_ffn_post = row(norm_ffn_pre), row(norm_ffn_post)
    bias_rows = _attn_bias_rows(attn_rel_bias)
    hg_consts = _hgrn_consts()
    hg_norm = row(hgrn_norm_g)
    gm_norm = row(gmlp_norm_g)
    gm_bias = jnp.repeat(jnp.swapaxes(gmlp_bs, 1, 2), HEAD_DIM, axis=2)
    lru_vec = jnp.concatenate(
        [lru_conv_w, lru_conv_b[:, None], lru_ba[:, None], lru_bx[:, None], lru_lambda[:, None]],
        axis=1)
    lru_w = _block_diag(jnp.stack([lru_wa, lru_wx], axis=1)).astype(BF16)
    for l in range(DEPTH):
        z_att, z_rest, w_gate, w_br, w_o = _in_proj(
            x2, l, g_mix_pre, w_mix,
            ((w_in, l, ((N_MIX_COLS, w_in.shape[2]),)), whole(w_branch_rows, l), whole(w_out, l)))
        z3 = z_rest.reshape(b, s, N_MIX_COLS - N_ATT_COLS)
        o_a = _attention(z_att.reshape(b, s, N_ATT_COLS), l, bias_rows)
        o_b, o_c = _hgrn2_gmlp(z3, l, hgrn_lb_logits, hg_norm, *hg_consts, gm_norm, gmlp_ws, gm_bias)
        o_d = _rg_lru(z3, l, lru_vec, lru_w)
        outs = [o.reshape(b * s, MIX_W) for o in (o_a, o_b, o_c, o_d)]
        x2, w1, w2 = _mix_out(x2, outs, l, g_mix_pre, g_mix_post,
                              w_gate, w_br.reshape(N_BRANCH, MIX_W, D_MODEL), w_o,
                              (whole(w_ffn_in, l), whole(w_ffn_out, l)))
        next_mix = ((w_in, l + 1, ((0, N_MIX_COLS),)),) if l + 1 < DEPTH else ()
        x2, *staged = _ffn(x2, l, g_ffn_pre, g_ffn_post, w1, w2, next_mix)
        if staged:
            (w_mix,) = staged
    return x2.reshape(b, s, d)
```

```python
import functools

import numpy as np
import jax
import jax.numpy as jnp
from jax import lax
from jax.experimental import pallas as pl
from jax.experimental.pallas import tpu as pltpu

D_MODEL = 1024
DEPTH = 4
CHUNK = 64
EPS = 1e-6
N_HEADS = 4
HEAD_DIM = 64
MIX_W = N_HEADS * HEAD_DIM
N_BRANCH = 4
ATT_LEFT_CHUNKS = 8
REL_MAX = 256
NEG_BIG = -1e30
LOG_FLOOR = 1e-30
GM_BLOCK = 128
CONV_W = 4
LRU_C = 8.0
FFN_HIDDEN = 2816
N_MIX_COLS = 11 * MIX_W
N_ATT_COLS = 3 * MIX_W
LOG2_E = 1.4426950408889634
ATT_Q_SCALE = HEAD_DIM ** -0.5 * LOG2_E

F32 = jnp.float32
BF16 = jnp.bfloat16

VMEM_LIMIT_BYTES = 56 * 1024 * 1024
TOKEN_TILE = 1024
SUB_TILE = 256
ATT_QB = 256
ATT_PAD = ATT_LEFT_CHUNKS * CHUNK
ATT_KB = ATT_PAD + ATT_QB
ATT_ROLL_W = 1024
MXU_TILE = 256
FFN_HIDDEN_CHUNKS = ((0, 6 * MXU_TILE), (6 * MXU_TILE, FFN_HIDDEN))
HG_CHUNK = 256
SUBLANES = 8
HG_BASE = 4
HG_LEVELS = (4, 8, 16, 32, 64, 128)


def _compiler_params(n_grid):
    return pltpu.CompilerParams(
        dimension_semantics=("arbitrary",) * n_grid,
        vmem_limit_bytes=VMEM_LIMIT_BYTES)


def _rms(x, g):
    return x * lax.rsqrt(jnp.mean(x * x, axis=-1, keepdims=True) + EPS) * g


def _gelu(x):
    c = 0.7978845608028654
    half_x = 0.5 * x
    return half_x + half_x * jnp.tanh(x * (c + (c * 0.044715) * (x * x)))


def _sigmoid(x):
    return 1.0 / (1.0 + jnp.exp(-x))


def _silu(x):
    return x * _sigmoid(x)


def _dot(a, b):
    return jnp.dot(a, b, preferred_element_type=F32)


def _dot_nt(a, b):
    return lax.dot_general(a, b, (((1,), (1,)), ((), ())), preferred_element_type=F32)


def _dot_tn(a, b):
    return lax.dot_general(a, b, (((0,), (0,)), ((), ())), preferred_element_type=F32)


def _split2(x):
    hi = x.astype(BF16)
    lo = (x - hi.astype(F32)).astype(BF16)
    return hi, lo


def _head_of_lane(shape, dim):
    return lax.broadcasted_iota(jnp.int32, shape, dim) // HEAD_DIM


def _resident_spec(shape):
    zeros = (0,) * len(shape)
    return pl.BlockSpec(tuple(shape), lambda i: zeros, pipeline_mode=pl.Buffered(1))


def _cast_specs(jobs, n_steps):
    in_specs, out_specs, out_shapes = [], [], []
    for stack, layer, col_ranges in jobs:
        _, n_rows, n_cols = stack.shape
        slab = n_rows // n_steps
        in_specs.append(pl.BlockSpec((None, slab, n_cols), lambda i, layer=layer: (layer, i, 0)))
        for lo, hi in col_ranges:
            out_specs.append(pl.BlockSpec((slab, hi - lo), lambda i: (i, 0)))
            out_shapes.append(jax.ShapeDtypeStruct((n_rows, hi - lo), BF16))
    return in_specs, out_specs, out_shapes


def _cast_slabs(col_ranges_per_job, src_refs, dst_refs):
    dst_refs = iter(dst_refs)
    for col_ranges, src_ref in zip(col_ranges_per_job, src_refs):
        for lo, hi in col_ranges:
            next(dst_refs)[...] = src_ref[:, lo:hi].astype(BF16)


def _layer_spec(shape, layer):
    zeros = (0,) * len(shape)
    return pl.BlockSpec((None,) + tuple(shape), lambda i: (layer,) + zeros,
                        pipeline_mode=pl.Buffered(1))


def _sub_tiles(n_rows):
    return tuple(pl.ds(r, SUB_TILE) for r in range(0, n_rows, SUB_TILE))


def _in_proj_kernel(cast_cols, x_ref, g_ref, w_ref, *refs):
    n_cast = len(cast_cols)
    za_ref, zr_ref = refs[n_cast:n_cast + 2]
    _cast_slabs(cast_cols, refs[:n_cast], refs[n_cast + 2:])
    for rows in _sub_tiles(x_ref.shape[0]):
        h = _rms(x_ref[rows, :], g_ref[...]).astype(BF16)
        z = _dot(h, w_ref[...])
        za_ref[rows, 0:MIX_W] = (z[:, 0:MIX_W] * ATT_Q_SCALE).astype(BF16)
        za_ref[rows, MIX_W:N_ATT_COLS] = z[:, MIX_W:N_ATT_COLS].astype(BF16)
        zr_ref[rows, :] = z[:, N_ATT_COLS:]


def _in_proj(x2, layer, g, w_mix, cast_jobs=()):
    t = x2.shape[0]
    n_rest = N_MIX_COLS - N_ATT_COLS
    n_steps = t // TOKEN_TILE
    cast_in, cast_out, cast_shapes = _cast_specs(cast_jobs, n_steps)
    return pl.pallas_call(
        functools.partial(_in_proj_kernel, tuple(job[2] for job in cast_jobs)),
        grid=(n_steps,),
        in_specs=[
            pl.BlockSpec((TOKEN_TILE, D_MODEL), lambda i: (i, 0)),
            _layer_spec((1, D_MODEL), layer),
            _resident_spec(w_mix.shape),
        ] + cast_in,
        out_specs=[pl.BlockSpec((TOKEN_TILE, N_ATT_COLS), lambda i: (i, 0)),
                   pl.BlockSpec((TOKEN_TILE, n_rest), lambda i: (i, 0))] + cast_out,
        out_shape=[jax.ShapeDtypeStruct((t, N_ATT_COLS), BF16),
                   jax.ShapeDtypeStruct((t, n_rest), F32)] + cast_shapes,
        compiler_params=_compiler_params(1),
        name="in_proj",
    )(x2, g, w_mix, *(job[0] for job in cast_jobs))


def _attn_kernel(q_ref, k_ref, v_ref, brow_ref, o_ref, bias_scr):
    s_len = q_ref.shape[0]

    @pl.when(pl.program_id(0) == 0)
    def _():
        qi = lax.broadcasted_iota(jnp.int32, (ATT_QB, ATT_KB), 0) // CHUNK
        kc = lax.broadcasted_iota(jnp.int32, (ATT_QB, ATT_KB), 1) // CHUNK
        band = (kc >= qi) & (kc <= qi + ATT_LEFT_CHUNKS)
        for h in range(N_HEADS):
            rows = jnp.broadcast_to(brow_ref[h:h + 1, :], (ATT_QB, ATT_ROLL_W))
            tile = pltpu.roll(rows, ATT_ROLL_W - ATT_QB, 1, stride=1, stride_axis=0)
            bias_scr[h] = jnp.where(band, tile[:, :ATT_KB], NEG_BIG)

    lane_head = _head_of_lane((ATT_QB, MIX_W), 1)
    head_rows = [jnp.where(_head_of_lane((1, MIX_W), 1) == h, 1.0, 0.0).astype(BF16)
                 for h in range(N_HEADS)]

    def block(r0, k0, n_keys):
        q = q_ref[pl.ds(r0, ATT_QB), :]
        kb = k_ref[pl.ds(k0, n_keys), :]
        vb = v_ref[pl.ds(k0, n_keys), :]
        acc = jnp.zeros((ATT_QB, MIX_W), F32)
        for h in range(N_HEADS):
            s = _dot_nt(q * head_rows[h], kb) + bias_scr[h, :, pl.ds(ATT_KB - n_keys, n_keys)]
            m = jnp.max(s, axis=-1, keepdims=True)
            p = jnp.exp2(s - m)
            l = jnp.sum(p, axis=-1, keepdims=True)
            o = _dot(p.astype(BF16), vb)
            acc = acc + jnp.where(lane_head == h, o * (1.0 / l), 0.0)
        o_ref[pl.ds(r0, ATT_QB), :] = acc.astype(o_ref.dtype)

    n_head_blocks = ATT_PAD // ATT_QB
    for j in range(n_head_blocks):
        block(j * ATT_QB, 0, (j + 1) * ATT_QB)

    def body(j, carry):
        r0 = pl.multiple_of(j * ATT_QB, ATT_QB)
        block(r0, r0 - ATT_PAD, ATT_KB)
        return carry

    lax.fori_loop(n_head_blocks, s_len // ATT_QB, body, 0, unroll=6)


def _attn_bias_rows(rel_bias):
    dist = ATT_PAD + ATT_QB - np.arange(ATT_ROLL_W)
    idx = np.clip(dist, -(CHUNK - 1), REL_MAX) + (CHUNK - 1)
    return rel_bias.astype(F32)[:, :, idx] * LOG2_E


def _attention(za3, layer, bias_rows):
    b, s, _ = za3.shape
    col_spec = lambda c: pl.BlockSpec((None, s, MIX_W), lambda i, c=c: (i, 0, c))
    return pl.pallas_call(
        _attn_kernel,
        grid=(b,),
        in_specs=[col_spec(0), col_spec(1), col_spec(2),
                  _layer_spec((N_HEADS, ATT_ROLL_W), layer)],
        out_specs=pl.BlockSpec((None, s, MIX_W), lambda i: (i, 0, 0)),
        out_shape=jax.ShapeDtypeStruct((b, s, MIX_W), BF16),
        scratch_shapes=[pltpu.VMEM((N_HEADS, ATT_QB, ATT_KB), F32)],
        compiler_params=_compiler_params(1),
        name="attention",
    )(za3, za3, za3, bias_rows)


def _hgrn_half_rows(g, right):
    off = g if right else 0
    return np.concatenate([np.arange(2 * g * b + off, 2 * g * b + off + g)
                           for b in range(HG_CHUNK // (2 * g))])


def _hgrn_coef_matrix():
    r = np.arange(HG_CHUNK)[None, :]
    mats = [r <= np.arange(HG_CHUNK)[:, None]]
    for g in HG_LEVELS:
        for right in (True, False):
            t = _hgrn_half_rows(g, right)[:, None]
            bnd = (t // (2 * g)) * (2 * g) + g - 1
            mats.append((r > bnd) & (r <= t) if right else (r > t) & (r <= bnd))
    return np.concatenate(mats, axis=0).astype(np.float32)


def _hgrn_level_masks():
    half = HG_CHUNK // 2
    tc = np.arange(half)[:, None]
    sc = np.arange(2 * half)[None, :] % half
    return np.stack([(tc // g) == (sc // g) for g in HG_LEVELS]).astype(np.float32)


def _hgrn_consts():
    return (jnp.asarray(_hgrn_coef_matrix(), dtype=BF16), jnp.asarray(_hgrn_level_masks()))


def _hgrn_gmlp_kernel(layer, q_ref, f_ref, i_ref, g_ref, gu_ref, gv_ref, lbl_ref, ng_ref,
                      coef_ref, lmask_ref, gm_ng_ref, gm_ws_ref, gm_bias_ref, o_ref, gm_o_ref):
    s_len = q_ref.shape[0]
    gm_w_cat = _gmlp_weights(gm_ws_ref)
    c = HG_CHUNK
    half = c // 2
    logits = lbl_ref[...]
    e = jnp.exp(logits - jnp.max(logits, axis=0, keepdims=True))
    p = e / jnp.sum(e, axis=0, keepdims=True)
    lb = jnp.zeros((1, MIX_W), F32)
    for l in range(1, layer + 1):
        lb = lb + p[l:l + 1, :]

    same_head = (_head_of_lane((MIX_W, MIX_W), 0) == _head_of_lane((MIX_W, MIX_W), 1))
    head_ones = jnp.where(same_head, 1.0, 0.0).astype(BF16)
    head_rows = [jnp.where(_head_of_lane((1, MIX_W), 1) == h, 1.0, 0.0).astype(BF16)
                 for h in range(N_HEADS)]
    group_shape = (c // SUBLANES, SUBLANES, MIX_W)
    row_in_group = lax.broadcasted_iota(jnp.int32, group_shape, 1) % HG_BASE

    def body(j, st):
        r0 = pl.multiple_of(j * c, c)
        fz = f_ref[pl.ds(r0, c), :]
        qf = _silu(q_ref[pl.ds(r0, c), :])
        sg = _sigmoid(fz)
        f = lb + (1.0 - lb) * sg
        lf = jnp.log2(jnp.maximum(f, LOG_FLOOR))
        kf = (1.0 - lb) * (1.0 - sg)
        v = i_ref[pl.ds(r0, c), :]
        vb = v.astype(BF16)

        lf_hi, lf_lo = _split2(lf)
        pc = _dot(coef_ref[0:c, :], lf_hi) + _dot(coef_ref[0:c, :], lf_lo)
        sums = _dot(coef_ref[c:, :], lf_hi)
        ptot = pc[c - 1:c, :]

        qd = (qf * jnp.exp2(pc)).astype(BF16)
        terms = [_dot_nt(qd, st.astype(BF16))]
        kd = (kf * jnp.exp2(ptot - pc)).astype(BF16)
        st_next = st * jnp.exp2(ptot) + jnp.where(same_head, _dot_tn(vb, kd), 0.0)

        p3 = pc.reshape(group_shape)
        q3, k3, v3 = (x.reshape(group_shape) for x in (qf, kf, v))

        def group_term(d):
            if d == 0:
                prod, vd = q3 * k3, v3
            else:
                e = jnp.exp2(jnp.minimum(p3 - pltpu.roll(p3, d, 1), 0.0))
                prod = jnp.where(row_in_group >= d, q3 * pltpu.roll(k3, d, 1) * e, 0.0)
                vd = pltpu.roll(v3, d, 1)
            w = _dot(prod.reshape(c, MIX_W).astype(BF16), head_ones)
            return w * vd.reshape(c, MIX_W)

        def level_term(li, g):
            n_blocks = half // g

            def half_rows(x, off):
                return jnp.concatenate(
                    [x[2 * g * b + off:2 * g * b + off + g] for b in range(n_blocks)], axis=0)

            cq = sums[li * c:li * c + half]
            ck = sums[li * c + half:(li + 1) * c]
            qt = (half_rows(qf, g) * jnp.exp2(cq)).astype(BF16)
            kt = (half_rows(kf, 0) * jnp.exp2(ck)).astype(BF16)
            vt = half_rows(v, 0).astype(BF16)
            out = jnp.zeros((half, MIX_W), F32)
            for h in range(0, N_HEADS, 2):
                kw = jnp.concatenate([kt * head_rows[h], kt * head_rows[h + 1]], axis=0)
                vw = jnp.concatenate([vt * head_rows[h], vt * head_rows[h + 1]], axis=0)
                sc = _dot_nt(qt, kw) * lmask_ref[li]
                out = out + _dot(sc.astype(BF16), vw)
            zero = jnp.zeros((g, MIX_W), F32)
            return jnp.concatenate(
                [piece for b in range(n_blocks) for piece in (zero, out[b * g:(b + 1) * g])], axis=0)

        gm_blocks = list(range(0, c, GM_BLOCK))
        for i in range(max(HG_BASE, len(HG_LEVELS))):
            if i < HG_BASE:
                terms.append(group_term(i))
            if i < len(HG_LEVELS):
                terms.append(level_term(i, HG_LEVELS[i]))
            if i % 2 == 1 and gm_blocks:
                _gmlp_block(r0 + gm_blocks.pop(0), gu_ref, gv_ref, gm_ng_ref, gm_bias_ref,
                            gm_o_ref, gm_w_cat, head_rows)
        while len(terms) > 1:
            terms = [a + b for a, b in zip(terms[0::2], terms[1::2])] + terms[len(terms) & ~1:]
        o = terms[0]

        ms = _dot((o * o).astype(BF16), head_ones) * (1.0 / HEAD_DIM)
        o = o * lax.rsqrt(ms + EPS) * ng_ref[...]
        o = o * _silu(g_ref[pl.ds(r0, c), :])
        o_ref[pl.ds(r0, c), :] = o.astype(o_ref.dtype)
        return st_next

    lax.fori_loop(0, s_len // c, body, jnp.zeros((MIX_W, MIX_W), F32), unroll=8)


def _hgrn2_gmlp(z3, layer, lb_logits, hg_norm, coef, level_masks, gm_norm, gm_ws, gm_bias):
    b, s, _ = z3.shape
    col_spec = lambda c: pl.BlockSpec((None, s, MIX_W), lambda i, c=c: (i, 0, c))
    out_spec = pl.BlockSpec((None, s, MIX_W), lambda i: (i, 0, 0))
    out_shape = jax.ShapeDtypeStruct((b, s, MIX_W), BF16)
    return pl.pallas_call(
        functools.partial(_hgrn_gmlp_kernel, layer),
        grid=(b,),
        in_specs=[col_spec(0), col_spec(1), col_spec(2), col_spec(3), col_spec(4), col_spec(5),
                  pl.BlockSpec((DEPTH, MIX_W), lambda i: (0, 0)),
                  _layer_spec((1, MIX_W), layer),
                  pl.BlockSpec(coef.shape, lambda i: (0, 0)),
                  pl.BlockSpec(level_masks.shape, lambda i: (0, 0, 0)),
                  _layer_spec((1, MIX_W), layer),
                  _layer_spec((N_HEADS, GM_BLOCK, GM_BLOCK), layer),
                  _layer_spec((GM_BLOCK, MIX_W), layer)],
        out_specs=[out_spec, out_spec],
        out_shape=[out_shape, out_shape],
        compiler_params=_compiler_params(1),
        name="hgrn2_gmlp",
    )(z3, z3, z3, z3, z3, z3, lb_logits, hg_norm, coef, level_masks, gm_norm, gm_ws, gm_bias)


def _gmlp_weights(ws_ref):
    prow = lax.broadcasted_iota(jnp.int32, (GM_BLOCK, GM_BLOCK), 0)
    pcol = lax.broadcasted_iota(jnp.int32, (GM_BLOCK, GM_BLOCK), 1)
    return jnp.concatenate(
        [jnp.where(pcol <= prow, ws_ref[g], 0.0).astype(BF16) for g in range(N_HEADS)], axis=1)


def _gmlp_block(r0, u_ref, v_ref, ng_ref, bias_ref, o_ref, w_cat, group_rows):
    rows = pl.ds(r0, GM_BLOCK)
    vn = _rms(_gelu(v_ref[rows, :]), ng_ref[...]).astype(BF16)
    vn_cat = jnp.concatenate([vn * group_rows[g] for g in range(N_HEADS)], axis=0)
    mixed = bias_ref[...] + _dot(w_cat, vn_cat)
    o_ref[rows, :] = (_gelu(u_ref[rows, :]) * mixed).astype(o_ref.dtype)


LRU_ROW_CONV_B, LRU_ROW_BA, LRU_ROW_BX, LRU_ROW_LAM = 4, 5, 6, 7
LRU_VEC_ROWS = 8


def _lru_kernel(x_ref, gate_ref, vec_ref, w_ref, o_ref, a_ref, b_ref):
    s_len = x_ref.shape[0]
    vrow = lambda k: vec_ref[k:k + 1, :]
    x = x_ref[...]
    t_head = lax.broadcasted_iota(jnp.int32, (SUBLANES, MIX_W), 0)

    def first_tile(fn, v):
        return jnp.concatenate([fn(v[0:SUBLANES]), v[SUBLANES:]], axis=0)

    xc = vrow(LRU_ROW_CONV_B) + x * vrow(CONV_W - 1)
    for d in range(1, CONV_W):
        xs = first_tile(lambda v, d=d: jnp.where(t_head >= d, v, 0.0), pltpu.roll(x, d, 0))
        xc = xc + xs * vrow(CONV_W - 1 - d)
    xcb = xc.astype(BF16)
    r = _sigmoid(_dot(xcb, w_ref[0]) + vrow(LRU_ROW_BA))
    ig = _sigmoid(_dot(xcb, w_ref[1]) + vrow(LRU_ROW_BX))
    nl = -vrow(LRU_ROW_LAM)
    softplus = jnp.maximum(nl, 0.0) + jnp.log(1.0 + jnp.exp(-jnp.abs(nl)))
    a_full = jnp.exp(r * (-LRU_C * softplus))
    mult = jnp.sqrt(jnp.maximum(1.0 - a_full * a_full, 0.0))
    mult = first_tile(lambda v: jnp.where(t_head == 0, 1.0, v), mult)
    tiles = (s_len // SUBLANES, SUBLANES, MIX_W)
    a = a_full.reshape(tiles)
    b = (mult * (ig * xc)).reshape(tiles)
    row_in_tile = lax.broadcasted_iota(jnp.int32, tiles, 1)
    d = 1
    while d < SUBLANES:
        ok = row_in_tile >= d
        a_prev = jnp.where(ok, pltpu.roll(a, d, 1), 1.0)
        b_prev = jnp.where(ok, pltpu.roll(b, d, 1), 0.0)
        b = a * b_prev + b
        a = a * a_prev
        d *= 2
    a_ref[...] = a.reshape(s_len, MIX_W)
    b_ref[...] = b.reshape(s_len, MIX_W)
    carry = jnp.zeros((1, MIX_W), F32)
    for r0 in range(0, s_len, SUBLANES):
        rows = pl.ds(r0, SUBLANES)
        h = b_ref[rows, :] + a_ref[rows, :] * carry
        b_ref[rows, :] = h
        carry = h[SUBLANES - 1:SUBLANES, :]
    o_ref[...] = (b_ref[...] * _gelu(gate_ref[...])).astype(o_ref.dtype)


def _block_diag(w):
    eye = jnp.eye(N_HEADS, dtype=w.dtype)
    bd = jnp.einsum('...hij,hg->...higj', w, eye)
    return bd.reshape(w.shape[:-3] + (MIX_W, MIX_W))


def _rg_lru(z3, layer, vec, w_bd):
    b, s, _ = z3.shape
    col_spec = lambda c: pl.BlockSpec((None, s, MIX_W), lambda i, c=c: (i, 0, c))
    return pl.pallas_call(
        _lru_kernel,
        grid=(b,),
        in_specs=[col_spec(6), col_spec(7),
                  _layer_spec((LRU_VEC_ROWS, MIX_W), layer),
                  _layer_spec((2, MIX_W, MIX_W), layer)],
        out_specs=pl.BlockSpec((None, s, MIX_W), lambda i: (i, 0, 0)),
        out_shape=jax.ShapeDtypeStruct((b, s, MIX_W), BF16),
        scratch_shapes=[pltpu.VMEM((s, MIX_W), F32), pltpu.VMEM((s, MIX_W), F32)],
        compiler_params=_compiler_params(1),
        name="rg_lru",
    )(z3, z3, vec, w_bd)


def _mix_out_kernel(cast_cols, x_ref, oa_ref, ob_ref, oc_ref, od_ref, gpre_ref, gpost_ref,
                    wg_ref, wb_ref, wo_ref, *refs):
    n_cast = len(cast_cols)
    out_ref = refs[n_cast]
    _cast_slabs(cast_cols, refs[:n_cast], refs[n_cast + 1:])
    for rows in _sub_tiles(x_ref.shape[0]):
        x = x_ref[rows, :]
        h = _rms(x, gpre_ref[...]).astype(BF16)
        merged = jnp.zeros(x.shape, F32)
        for n, o_ref in enumerate((oa_ref, ob_ref, oc_ref, od_ref)):
            gate = _dot(h, wg_ref[:, n * D_MODEL:(n + 1) * D_MODEL])
            proj = _dot(o_ref[rows, :], wb_ref[n])
            merged = merged + _sigmoid(gate) * proj
        y = _dot(merged.astype(BF16), wo_ref[...])
        out_ref[rows, :] = x + _rms(y, gpost_ref[...])


def _mix_out(x2, outs, layer, g_pre, g_post, w_gate, w_branch, w_out, cast_jobs=()):
    t = x2.shape[0]
    n_steps = t // TOKEN_TILE
    tok = lambda w: pl.BlockSpec((TOKEN_TILE, w), lambda i: (i, 0))
    cast_in, cast_out, cast_shapes = _cast_specs(cast_jobs, n_steps)
    return pl.pallas_call(
        functools.partial(_mix_out_kernel, tuple(job[2] for job in cast_jobs)),
        grid=(n_steps,),
        in_specs=[tok(D_MODEL), tok(MIX_W), tok(MIX_W), tok(MIX_W), tok(MIX_W),
                  _layer_spec((1, D_MODEL), layer), _layer_spec((1, D_MODEL), layer),
                  _resident_spec(w_gate.shape), _resident_spec(w_branch.shape),
                  _resident_spec(w_out.shape)] + cast_in,
        out_specs=[tok(D_MODEL)] + cast_out,
        out_shape=[jax.ShapeDtypeStruct((t, D_MODEL), F32)] + cast_shapes,
        compiler_params=_compiler_params(1),
        name="mix_out",
    )(x2, *outs, g_pre, g_post, w_gate, w_branch, w_out, *(job[0] for job in cast_jobs))


def _ffn_kernel(cast_cols, x_ref, gpre_ref, gpost_ref, w1_ref, w2_ref, *refs):
    n_cast = len(cast_cols)
    out_ref = refs[n_cast]
    _cast_slabs(cast_cols, refs[:n_cast], refs[n_cast + 1:])
    for rows in _sub_tiles(x_ref.shape[0]):
        x = x_ref[rows, :]
        h = _rms(x, gpre_ref[...]).astype(BF16)
        y = jnp.zeros(x.shape, F32)
        for lo, hi in FFN_HIDDEN_CHUNKS:
            gt = _dot(h, w1_ref[:, lo:hi])
            up = _dot(h, w1_ref[:, FFN_HIDDEN + lo:FFN_HIDDEN + hi])
            act = (_silu(gt) * up).astype(BF16)
            y = y + _dot(act, w2_ref[lo:hi, :])
        out_ref[rows, :] = x + _rms(y, gpost_ref[...])


def _ffn(x2, layer, g_pre, g_post, w1, w2, cast_jobs=()):
    t = x2.shape[0]
    n_steps = t // TOKEN_TILE
    tok = pl.BlockSpec((TOKEN_TILE, D_MODEL), lambda i: (i, 0))
    cast_in, cast_out, cast_shapes = _cast_specs(cast_jobs, n_steps)
    return pl.pallas_call(
        functools.partial(_ffn_kernel, tuple(job[2] for job in cast_jobs)),
        grid=(n_steps,),
        in_specs=[tok, _layer_spec((1, D_MODEL), layer), _layer_spec((1, D_MODEL), layer),
                  _resident_spec(w1.shape), _resident_spec(w2.shape)] + cast_in,
        out_specs=[tok] + cast_out,
        out_shape=[jax.ShapeDtypeStruct((t, D_MODEL), F32)] + cast_shapes,
        compiler_params=_compiler_params(1),
        name="ffn",
    )(x2, g_pre, g_post, w1, w2, *(job[0] for job in cast_jobs))


def kernel(x, norm_mix_pre, norm_mix_post, norm_ffn_pre, norm_ffn_post, w_in, attn_rel_bias, hgrn_lb_logits, hgrn_norm_g, gmlp_norm_g, gmlp_ws, gmlp_bs, lru_conv_w, lru_conv_b, lru_wa, lru_ba, lru_wx, lru_bx, lru_lambda, w_branch, w_out, w_ffn_in, w_ffn_out):
    b, s, d = x.shape
    x2 = x.reshape(b * s, d)
    w_mix = w_in[0, :, :N_MIX_COLS].astype(BF16)
    w_branch_rows = w_branch.reshape(DEPTH, N_BRANCH * MIX_W, D_MODEL)
    whole = lambda stack, layer: (stack, layer, ((0, stack.shape[2]),))
    row = lambda p: p.reshape(DEPTH, 1, p.shape[-1])
    g_mix_pre, g_mix_post = row(norm_mix_pre), row(norm_mix_post)
    g_ffn_pre, g_ffn_post = row(norm_ffn_pre), row(norm_ffn_post)
    bias_rows = _attn_bias_rows(attn_rel_bias)
    hg_consts = _hgrn_consts()
    hg_norm = row(hgrn_norm_g)
    gm_norm = row(gmlp_norm_g)
    gm_bias = jnp.repeat(jnp.swapaxes(gmlp_bs, 1, 2), HEAD_DIM, axis=2)
    lru_vec = jnp.concatenate(
        [lru_conv_w, lru_conv_b[:, None], lru_ba[:, None], lru_bx[:, None], lru_lambda[:, None]],
        axis=1)
    lru_w = _block_diag(jnp.stack([lru_wa, lru_wx], axis=1)).astype(BF16)
    for l in range(DEPTH):
        z_att, z_rest, w_gate, w_br, w_o = _in_proj(
            x2, l, g_mix_pre, w_mix,
            ((w_in, l, ((N_MIX_COLS, w_in.shape[2]),)), whole(w_branch_rows, l), whole(w_out, l)))
        z3 = z_rest.reshape(b, s, N_MIX_COLS - N_ATT_COLS)
        o_a = _attention(z_att.reshape(b, s, N_ATT_COLS), l, bias_rows)
        o_b, o_c = _hgrn2_gmlp(z3, l, hgrn_lb_logits, hg_norm, *hg_consts, gm_norm, gmlp_ws, gm_bias)
        o_d = _rg_lru(z3, l, lru_vec, lru_w)
        outs = [o.reshape(b * s, MIX_W) for o in (o_a, o_b, o_c, o_d)]
        x2, w1, w2 = _mix_out(x2, outs, l, g_mix_pre, g_mix_post,
                              w_gate, w_br.reshape(N_BRANCH, MIX_W, D_MODEL), w_o,
                              (whole(w_ffn_in, l), whole(w_ffn_out, l)))
        next_mix = ((w_in, l + 1, ((0, N_MIX_COLS),)),) if l + 1 < DEPTH else ()
        x2, *staged = _ffn(x2, l, g_ffn_pre, g_ffn_post, w1, w2, next_mix)
        if staged:
            (w_mix,) = staged
    return x2.reshape(b, s, d)
```

```python
import functools

import numpy as np
import jax
import jax.numpy as jnp
from jax import lax
from jax.experimental import pallas as pl
from jax.experimental.pallas import tpu as pltpu

D_MODEL = 1024
DEPTH = 4
CHUNK = 64
EPS = 1e-6
N_HEADS = 4
HEAD_DIM = 64
MIX_W = N_HEADS * HEAD_DIM
N_BRANCH = 4
ATT_LEFT_CHUNKS = 8
REL_MAX = 256
NEG_BIG = -1e30
LOG_FLOOR = 1e-30
GM_BLOCK = 128
CONV_W = 4
LRU_C = 8.0
FFN_HIDDEN = 2816
N_MIX_COLS = 11 * MIX_W
N_ATT_COLS = 3 * MIX_W
LOG2_E = 1.4426950408889634
ATT_Q_SCALE = HEAD_DIM ** -0.5 * LOG2_E

F32 = jnp.float32
BF16 = jnp.bfloat16

VMEM_LIMIT_BYTES = 56 * 1024 * 1024
TOKEN_TILE = 1024
SUB_TILE = 256
ATT_QB = 256
ATT_PAD = ATT_LEFT_CHUNKS * CHUNK
ATT_KB = ATT_PAD + ATT_QB
ATT_ROLL_W = 1024
MXU_TILE = 256
FFN_HIDDEN_CHUNKS = ((0, 6 * MXU_TILE), (6 * MXU_TILE, FFN_HIDDEN))
HG_CHUNK = 256
SUBLANES = 8
HG_BASE = 4
HG_LEVELS = (4, 8, 16, 32, 64, 128)


def _compiler_params(n_grid):
    return pltpu.CompilerParams(
        dimension_semantics=("arbitrary",) * n_grid,
        vmem_limit_bytes=VMEM_LIMIT_BYTES)


def _rms(x, g):
    return x * lax.rsqrt(jnp.mean(x * x, axis=-1, keepdims=True) + EPS) * g


def _gelu(x):
    c = 0.7978845608028654
    half_x = 0.5 * x
    return half_x + half_x * jnp.tanh(x * (c + (c * 0.044715) * (x * x)))


def _sigmoid(x):
    return 1.0 / (1.0 + jnp.exp(-x))


def _silu(x):
    return x * _sigmoid(x)


def _dot(a, b):
    return jnp.dot(a, b, preferred_element_type=F32)


def _dot_nt(a, b):
    return lax.dot_general(a, b, (((1,), (1,)), ((), ())), preferred_element_type=F32)


def _dot_tn(a, b):
    return lax.dot_general(a, b, (((0,), (0,)), ((), ())), preferred_element_type=F32)


def _split2(x):
    hi = x.astype(BF16)
    lo = (x - hi.astype(F32)).astype(BF16)
    return hi, lo


def _head_of_lane(shape, dim):
    return lax.broadcasted_iota(jnp.int32, shape, dim) // HEAD_DIM


def _resident_spec(shape):
    zeros = (0,) * len(shape)
    return pl.BlockSpec(tuple(shape), lambda i: zeros, pipeline_mode=pl.Buffered(1))


def _cast_specs(jobs, n_steps):
    in_specs, out_specs, out_shapes = [], [], []
    for stack, layer, col_ranges in jobs:
        _, n_rows, n_cols = stack.shape
        slab = n_rows // n_steps
        in_specs.append(pl.BlockSpec((None, slab, n_cols), lambda i, layer=layer: (layer, i, 0)))
        for lo, hi in col_ranges:
            out_specs.append(pl.BlockSpec((slab, hi - lo), lambda i: (i, 0)))
            out_shapes.append(jax.ShapeDtypeStruct((n_rows, hi - lo), BF16))
    return in_specs, out_specs, out_shapes


def _cast_slabs(col_ranges_per_job, src_refs, dst_refs):
    dst_refs = iter(dst_refs)
    for col_ranges, src_ref in zip(col_ranges_per_job, src_refs):
        for lo, hi in col_ranges:
            next(dst_refs)[...] = src_ref[:, lo:hi].astype(BF16)


def _layer_spec(shape, layer):
    zeros = (0,) * len(shape)
    return pl.BlockSpec((None,) + tuple(shape), lambda i: (layer,) + zeros,
                        pipeline_mode=pl.Buffered(1))


def _sub_tiles(n_rows):
    return tuple(pl.ds(r, SUB_TILE) for r in range(0, n_rows, SUB_TILE))


def _in_proj_kernel(cast_cols, x_ref, g_ref, w_ref, *refs):
    n_cast = len(cast_cols)
    za_ref, zr_ref = refs[n_cast:n_cast + 2]
    _cast_slabs(cast_cols, refs[:n_cast], refs[n_cast + 2:])
    for rows in _sub_tiles(x_ref.shape[0]):
        h = _rms(x_ref[rows, :], g_ref[...]).astype(BF16)
        z = _dot(h, w_ref[...])
        za_ref[rows, 0:MIX_W] = (z[:, 0:MIX_W] * ATT_Q_SCALE).astype(BF16)
        za_ref[rows, MIX_W:N_ATT_COLS] = z[:, MIX_W:N_ATT_COLS].astype(BF16)
        zr_ref[rows, :] = z[:, N_ATT_COLS:]


def _in_proj(x2, layer, g, w_mix, cast_jobs=()):
    t = x2.shape[0]
    n_rest = N_MIX_COLS - N_ATT_COLS
    n_steps = t // TOKEN_TILE
    cast_in, cast_out, cast_shapes = _cast_specs(cast_jobs, n_steps)
    return pl.pallas_call(
        functools.partial(_in_proj_kernel, tuple(job[2] for job in cast_jobs)),
        grid=(n_steps,),
        in_specs=[
            pl.BlockSpec((TOKEN_TILE, D_MODEL), lambda i: (i, 0)),
            _layer_spec((1, D_MODEL), layer),
            _resident_spec(w_mix.shape),
        ] + cast_in,
        out_specs=[pl.BlockSpec((TOKEN_TILE, N_ATT_COLS), lambda i: (i, 0)),
                   pl.BlockSpec((TOKEN_TILE, n_rest), lambda i: (i, 0))] + cast_out,
        out_shape=[jax.ShapeDtypeStruct((t, N_ATT_COLS), BF16),
                   jax.ShapeDtypeStruct((t, n_rest), F32)] + cast_shapes,
        compiler_params=_compiler_params(1),
        name="in_proj",
    )(x2, g, w_mix, *(job[0] for job in cast_jobs))


def _attn_kernel(q_ref, k_ref, v_ref, brow_ref, o_ref, bias_scr):
    s_len = q_ref.shape[0]

    @pl.when(pl.program_id(0) == 0)
    def _():
        qi = lax.broadcasted_iota(jnp.int32, (ATT_QB, ATT_KB), 0) // CHUNK
        kc = lax.broadcasted_iota(jnp.int32, (ATT_QB, ATT_KB), 1) // CHUNK
        band = (kc >= qi) & (kc <= qi + ATT_LEFT_CHUNKS)
        for h in range(N_HEADS):
            rows = jnp.broadcast_to(brow_ref[h:h + 1, :], (ATT_QB, ATT_ROLL_W))
            tile = pltpu.roll(rows, ATT_ROLL_W - ATT_QB, 1, stride=1, stride_axis=0)
            bias_scr[h] = jnp.where(band, tile[:, :ATT_KB], NEG_BIG)

    lane_head = _head_of_lane((ATT_QB, MIX_W), 1)
    head_rows = [jnp.where(_head_of_lane((1, MIX_W), 1) == h, 1.0, 0.0).astype(BF16)
                 for h in range(N_HEADS)]

    def block(r0, k0, n_keys):
        q = q_ref[pl.ds(r0, ATT_QB), :]
        kb = k_ref[pl.ds(k0, n_keys), :]
        vb = v_ref[pl.ds(k0, n_keys), :]
        acc = jnp.zeros((ATT_QB, MIX_W), F32)
        for h in range(N_HEADS):
            s = _dot_nt(q * head_rows[h], kb) + bias_scr[h, :, pl.ds(ATT_KB - n_keys, n_keys)]
            m = jnp.max(s, axis=-1, keepdims=True)
            p = jnp.exp2(s - m)
            l = jnp.sum(p, axis=-1, keepdims=True)
            o = _dot(p.astype(BF16), vb)
            acc = acc + jnp.where(lane_head == h, o * (1.0 / l), 0.0)
        o_ref[pl.ds(r0, ATT_QB), :] = acc.astype(o_ref.dtype)

    n_head_blocks = ATT_PAD // ATT_QB
    for j in range(n_head_blocks):
        block(j * ATT_QB, 0, (j + 1) * ATT_QB)

    def body(j, carry):
        r0 = pl.multiple_of(j * ATT_QB, ATT_QB)
        block(r0, r0 - ATT_PAD, ATT_KB)
        return carry

    lax.fori_loop(n_head_blocks, s_len // ATT_QB, body, 0, unroll=6)


def _attn_bias_rows(rel_bias):
    dist = ATT_PAD + ATT_QB - np.arange(ATT_ROLL_W)
    idx = np.clip(dist, -(CHUNK - 1), REL_MAX) + (CHUNK - 1)
    return rel_bias.astype(F32)[:, :, idx] * LOG2_E


def _attention(za3, layer, bias_rows):
    b, s, _ = za3.shape
    col_spec = lambda c: pl.BlockSpec((None, s, MIX_W), lambda i, c=c: (i, 0, c))
    return pl.pallas_call(
        _attn_kernel,
        grid=(b,),
        in_specs=[col_spec(0), col_spec(1), col_spec(2),
                  _layer_spec((N_HEADS, ATT_ROLL_W), layer)],
        out_specs=pl.BlockSpec((None, s, MIX_W), lambda i: (i, 0, 0)),
        out_shape=jax.ShapeDtypeStruct((b, s, MIX_W), BF16),
        scratch_shapes=[pltpu.VMEM((N_HEADS, ATT_QB, ATT_KB), F32)],
        compiler_params=_compiler_params(1),
        name="attention",
    )(za3, za3, za3, bias_rows)


def _hgrn_half_rows(g, right):
    off = g if right else 0
    return np.concatenate([np.arange(2 * g * b + off, 2 * g * b + off + g)
                           for b in range(HG_CHUNK // (2 * g))])


def _hgrn_coef_matrix():
    r = np.arange(HG_CHUNK)[None, :]
    mats = [r <= np.arange(HG_CHUNK)[:, None]]
    for g in HG_LEVELS:
        for right in (True, False):
            t = _hgrn_half_rows(g, right)[:, None]
            bnd = (t // (2 * g)) * (2 * g) + g - 1
            mats.append((r > bnd) & (r <= t) if right else (r > t) & (r <= bnd))
    return np.concatenate(mats, axis=0).astype(np.float32)


def _hgrn_level_masks():
    half = HG_CHUNK // 2
    tc = np.arange(half)[:, None]
    sc = np.arange(2 * half)[None, :] % half
    return np.stack([(tc // g) == (sc // g) for g in HG_LEVELS]).astype(np.float32)


def _hgrn_consts():
    return (jnp.asarray(_hgrn_coef_matrix(), dtype=BF16), jnp.asarray(_hgrn_level_masks()))


def _hgrn_gmlp_kernel(layer, q_ref, f_ref, i_ref, g_ref, gu_ref, gv_ref, lbl_ref, ng_ref,
                      coef_ref, lmask_ref, gm_ng_ref, gm_ws_ref, gm_bias_ref, o_ref, gm_o_ref):
    s_len = q_ref.shape[0]
    gm_w_cat = _gmlp_weights(gm_ws_ref)
    c = HG_CHUNK
    half = c // 2
    logits = lbl_ref[...]
    e = jnp.exp(logits - jnp.max(logits, axis=0, keepdims=True))
    p = e / jnp.sum(e, axis=0, keepdims=True)
    lb = jnp.zeros((1, MIX_W), F32)
    for l in range(1, layer + 1):
        lb = lb + p[l:l + 1, :]

    same_head = (_head_of_lane((MIX_W, MIX_W), 0) == _head_of_lane((MIX_W, MIX_W), 1))
    head_ones = jnp.where(same_head, 1.0, 0.0).astype(BF16)
    head_rows = [jnp.where(_head_of_lane((1, MIX_W), 1) == h, 1.0, 0.0).astype(BF16)
                 for h in range(N_HEADS)]
    group_shape = (c // SUBLANES, SUBLANES, MIX_W)
    row_in_group = lax.broadcasted_iota(jnp.int32, group_shape, 1) % HG_BASE

    def body(j, st):
        r0 = pl.multiple_of(j * c, c)
        fz = f_ref[pl.ds(r0, c), :]
        qf = _silu(q_ref[pl.ds(r0, c), :])
        sg = _sigmoid(fz)
        f = lb + (1.0 - lb) * sg
        lf = jnp.log2(jnp.maximum(f, LOG_FLOOR))
        kf = (1.0 - lb) * (1.0 - sg)
        v = i_ref[pl.ds(r0, c), :]
        vb = v.astype(BF16)

        lf_hi, lf_lo = _split2(lf)
        pc = _dot(coef_ref[0:c, :], lf_hi) + _dot(coef_ref[0:c, :], lf_lo)
        sums = _dot(coef_ref[c:, :], lf_hi)
        ptot = pc[c - 1:c, :]

        qd = (qf * jnp.exp2(pc)).astype(BF16)
        terms = [_dot_nt(qd, st.astype(BF16))]
        kd = (kf * jnp.exp2(ptot - pc)).astype(BF16)
        st_next = st * jnp.exp2(ptot) + jnp.where(same_head, _dot_tn(vb, kd), 0.0)

        p3 = pc.reshape(group_shape)
        q3, k3, v3 = (x.reshape(group_shape) for x in (qf, kf, v))

        def group_term(d):
            if d == 0:
                prod, vd = q3 * k3, v3
            else:
                e = jnp.exp2(jnp.minimum(p3 - pltpu.roll(p3, d, 1), 0.0))
                prod = jnp.where(row_in_group >= d, q3 * pltpu.roll(k3, d, 1) * e, 0.0)
                vd = pltpu.roll(v3, d, 1)
            w = _dot(prod.reshape(c, MIX_W).astype(BF16), head_ones)
            return w * vd.reshape(c, MIX_W)

        def level_term(li, g):
            n_blocks = half // g

            def half_rows(x, off):
                return jnp.concatenate(
                    [x[2 * g * b + off:2 * g * b + off + g] for b in range(n_blocks)], axis=0)

            cq = sums[li * c:li * c + half]
            ck = sums[li * c + half:(li + 1) * c]
            qt = (half_rows(qf, g) * jnp.exp2(cq)).astype(BF16)
            kt = (half_rows(kf, 0) * jnp.exp2(ck)).astype(BF16)
            vt = half_rows(v, 0).astype(BF16)
            out = jnp.zeros((half, MIX_W), F32)
            for h in range(0, N_HEADS, 2):
                kw = jnp.concatenate([kt * head_rows[h], kt * head_rows[h + 1]], axis=0)
                vw = jnp.concatenate([vt * head_rows[h], vt * head_rows[h + 1]], axis=0)
                sc = _dot_nt(qt, kw) * lmask_ref[li]
                out = out + _dot(sc.astype(BF16), vw)
            zero = jnp.zeros((g, MIX_W), F32)
            return jnp.concatenate(
                [piece for b in range(n_blocks) for piece in (zero, out[b * g:(b + 1) * g])], axis=0)

        gm_blocks = list(range(0, c, GM_BLOCK))
        for i in range(max(HG_BASE, len(HG_LEVELS))):
            if i < HG_BASE:
                terms.append(group_term(i))
            if i < len(HG_LEVELS):
                terms.append(level_term(i, HG_LEVELS[i]))
            if i % 2 == 1 and gm_blocks:
                _gmlp_block(r0 + gm_blocks.pop(0), gu_ref, gv_ref, gm_ng_ref, gm_bias_ref,
                            gm_o_ref, gm_w_cat, head_rows)
        while len(terms) > 1:
            terms = [a + b for a, b in zip(terms[0::2], terms[1::2])] + terms[len(terms) & ~1:]
        o = terms[0]

        ms = _dot((o * o).astype(BF16), head_ones) * (1.0 / HEAD_DIM)
        o = o * lax.rsqrt(ms + EPS) * ng_ref[...]
        o = o * _silu(g_ref[pl.ds(r0, c), :])
        o_ref[pl.ds(r0, c), :] = o.astype(o_ref.dtype)
        return st_next

    lax.fori_loop(0, s_len // c, body, jnp.zeros((MIX_W, MIX_W), F32), unroll=8)


def _hgrn2_gmlp(z3, layer, lb_logits, hg_norm, coef, level_masks, gm_norm, gm_ws, gm_bias):
    b, s, _ = z3.shape
    col_spec = lambda c: pl.BlockSpec((None, s, MIX_W), lambda i, c=c: (i, 0, c))
    out_spec = pl.BlockSpec((None, s, MIX_W), lambda i: (i, 0, 0))
    out_shape = jax.ShapeDtypeStruct((b, s, MIX_W), BF16)
    return pl.pallas_call(
        functools.partial(_hgrn_gmlp_kernel, layer),
        grid=(b,),
        in_specs=[col_spec(0), col_spec(1), col_spec(2), col_spec(3), col_spec(4), col_spec(5),
                  pl.BlockSpec((DEPTH, MIX_W), lambda i: (0, 0)),
                  _layer_spec((1, MIX_W), layer),
                  pl.BlockSpec(coef.shape, lambda i: (0, 0)),
                  pl.BlockSpec(level_masks.shape, lambda i: (0, 0, 0)),
                  _layer_spec((1, MIX_W), layer),
                  _layer_spec((N_HEADS, GM_BLOCK, GM_BLOCK), layer),
                  _layer_spec((GM_BLOCK, MIX_W), layer)],
        out_specs=[out_spec, out_spec],
        out_shape=[out_shape, out_shape],
        compiler_params=_compiler_params(1),
        name="hgrn2_gmlp",
    )(z3, z3, z3, z3, z3, z3, lb_logits, hg_norm, coef, level_masks, gm_norm, gm_ws, gm_bias)


def _gmlp_weights(ws_ref):
    prow = lax.broadcasted_iota(jnp.int32, (GM_BLOCK, GM_BLOCK), 0)
    pcol = lax.broadcasted_iota(jnp.int32, (GM_BLOCK, GM_BLOCK), 1)
    return jnp.concatenate(
        [jnp.where(pcol <= prow, ws_ref[g], 0.0).astype(BF16) for g in range(N_HEADS)], axis=1)


def _gmlp_block(r0, u_ref, v_ref, ng_ref, bias_ref, o_ref, w_cat, group_rows):
    rows = pl.ds(r0, GM_BLOCK)
    vn = _rms(_gelu(v_ref[rows, :]), ng_ref[...]).astype(BF16)
    vn_cat = jnp.concatenate([vn * group_rows[g] for g in range(N_HEADS)], axis=0)
    mixed = bias_ref[...] + _dot(w_cat, vn_cat)
    o_ref[rows, :] = (_gelu(u_ref[rows, :]) * mixed).astype(o_ref.dtype)


LRU_ROW_CONV_B, LRU_ROW_BA, LRU_ROW_BX, LRU_ROW_LAM = 4, 5, 6, 7
LRU_VEC_ROWS = 8


def _lru_kernel(x_ref, gate_ref, vec_ref, w_ref, o_ref, a_ref, b_ref):
    s_len = x_ref.shape[0]
    vrow = lambda k: vec_ref[k:k + 1, :]
    x = x_ref[...]
    t_head = lax.broadcasted_iota(jnp.int32, (SUBLANES, MIX_W), 0)

    def first_tile(fn, v):
        return jnp.concatenate([fn(v[0:SUBLANES]), v[SUBLANES:]], axis=0)

    xc = vrow(LRU_ROW_CONV_B) + x * vrow(CONV_W - 1)
    for d in range(1, CONV_W):
        xs = first_tile(lambda v, d=d: jnp.where(t_head >= d, v, 0.0), pltpu.roll(x, d, 0))
        xc = xc + xs * vrow(CONV_W - 1 - d)
    xcb = xc.astype(BF16)
    r = _sigmoid(_dot(xcb, w_ref[0]) + vrow(LRU_ROW_BA))
    ig = _sigmoid(_dot(xcb, w_ref[1]) + vrow(LRU_ROW_BX))
    nl = -vrow(LRU_ROW_LAM)
    softplus = jnp.maximum(nl, 0.0) + jnp.log(1.0 + jnp.exp(-jnp.abs(nl)))
    a_full = jnp.exp(r * (-LRU_C * softplus))
    mult = jnp.sqrt(jnp.maximum(1.0 - a_full * a_full, 0.0))
    mult = first_tile(lambda v: jnp.where(t_head == 0, 1.0, v), mult)
    tiles = (s_len // SUBLANES, SUBLANES, MIX_W)
    a = a_full.reshape(tiles)
    b = (mult * (ig * xc)).reshape(tiles)
    row_in_tile = lax.broadcasted_iota(jnp.int32, tiles, 1)
    d = 1
    while d < SUBLANES:
        ok = row_in_tile >= d
        a_prev = jnp.where(ok, pltpu.roll(a, d, 1), 1.0)
        b_prev = jnp.where(ok, pltpu.roll(b, d, 1), 0.0)
        b = a * b_prev + b
        a = a * a_prev
        d *= 2
    a_ref[...] = a.reshape(s_len, MIX_W)
    b_ref[...] = b.reshape(s_len, MIX_W)
    carry = jnp.zeros((1, MIX_W), F32)
    for r0 in range(0, s_len, SUBLANES):
        rows = pl.ds(r0, SUBLANES)
        h = b_ref[rows, :] + a_ref[rows, :] * carry
        b_ref[rows, :] = h
        carry = h[SUBLANES - 1:SUBLANES, :]
    o_ref[...] = (b_ref[...] * _gelu(gate_ref[...])).astype(o_ref.dtype)


def _block_diag(w):
    eye = jnp.eye(N_HEADS, dtype=w.dtype)
    bd = jnp.einsum('...hij,hg->...higj', w, eye)
    return bd.reshape(w.shape[:-3] + (MIX_W, MIX_W))


def _rg_lru(z3, layer, vec, w_bd):
    b, s, _ = z3.shape
    col_spec = lambda c: pl.BlockSpec((None, s, MIX_W), lambda i, c=c: (i, 0, c))
    return pl.pallas_call(
        _lru_kernel,
        grid=(b,),
        in_specs=[col_spec(6), col_spec(7),
                  _layer_spec((LRU_VEC_ROWS, MIX_W), layer),
                  _layer_spec((2, MIX_W, MIX_W), layer)],
        out_specs=pl.BlockSpec((None, s, MIX_W), lambda i: (i, 0, 0)),
        out_shape=jax.ShapeDtypeStruct((b, s, MIX_W), BF16),
        scratch_shapes=[pltpu.VMEM((s, MIX_W), F32), pltpu.VMEM((s, MIX_W), F32)],
        compiler_params=_compiler_params(1),
        name="rg_lru",
    )(z3, z3, vec, w_bd)


def _mix_out_kernel(cast_cols, x_ref, oa_ref, ob_ref, oc_ref, od_ref, gpre_ref, gpost_ref,
                    wg_ref, wb_ref, wo_ref, *refs):
    n_cast = len(cast_cols)
    out_ref = refs[n_cast]
    _cast_slabs(cast_cols, refs[:n_cast], refs[n_cast + 1:])
    for rows in _sub_tiles(x_ref.shape[0]):
        x = x_ref[rows, :]
        h = _rms(x, gpre_ref[...]).astype(BF16)
        merged = jnp.zeros(x.shape, F32)
        for n, o_ref in enumerate((oa_ref, ob_ref, oc_ref, od_ref)):
            gate = _dot(h, wg_ref[:, n * D_MODEL:(n + 1) * D_MODEL])
            proj = _dot(o_ref[rows, :], wb_ref[n])
            merged = merged + _sigmoid(gate) * proj
        y = _dot(merged.astype(BF16), wo_ref[...])
        out_ref[rows, :] = x + _rms(y, gpost_ref[...])


def _mix_out(x2, outs, layer, g_pre, g_post, w_gate, w_branch, w_out, cast_jobs=()):
    t = x2.shape[0]
    n_steps = t // TOKEN_TILE
    tok = lambda w: pl.BlockSpec((TOKEN_TILE, w), lambda i: (i, 0))
    cast_in, cast_out, cast_shapes = _cast_specs(cast_jobs, n_steps)
    return pl.pallas_call(
        functools.partial(_mix_out_kernel, tuple(job[2] for job in cast_jobs)),
        grid=(n_steps,),
        in_specs=[tok(D_MODEL), tok(MIX_W), tok(MIX_W), tok(MIX_W), tok(MIX_W),
                  _layer_spec((1, D_MODEL), layer), _layer_spec((1, D_MODEL), layer),
                  _resident_spec(w_gate.shape), _resident_spec(w_branch.shape),
                  _resident_spec(w_out.shape)] + cast_in,
        out_specs=[tok(D_MODEL)] + cast_out,
        out_shape=[jax.ShapeDtypeStruct((t, D_MODEL), F32)] + cast_shapes,
        compiler_params=_compiler_params(1),
        name="mix_out",
    )(x2, *outs, g_pre, g_post, w_gate, w_branch, w_out, *(job[0] for job in cast_jobs))


def _ffn_kernel(cast_cols, x_ref, gpre_ref, gpost_ref, w1_ref, w2_ref, *refs):
    n_cast = len(cast_cols)
    out_ref = refs[n_cast]
    _cast_slabs(cast_cols, refs[:n_cast], refs[n_cast + 1:])
    for rows in _sub_tiles(x_ref.shape[0]):
        x = x_ref[rows, :]
        h = _rms(x, gpre_ref[...]).astype(BF16)
        y = jnp.zeros(x.shape, F32)
        for lo, hi in FFN_HIDDEN_CHUNKS:
            gt = _dot(h, w1_ref[:, lo:hi])
            up = _dot(h, w1_ref[:, FFN_HIDDEN + lo:FFN_HIDDEN + hi])
            act = (_silu(gt) * up).astype(BF16)
            y = y + _dot(act, w2_ref[lo:hi, :])
        out_ref[rows, :] = x + _rms(y, gpost_ref[...])


def _ffn(x2, layer, g_pre, g_post, w1, w2, cast_jobs=()):
    t = x2.shape[0]
    n_steps = t // TOKEN_TILE
    tok = pl.BlockSpec((TOKEN_TILE, D_MODEL), lambda i: (i, 0))
    cast_in, cast_out, cast_shapes = _cast_specs(cast_jobs, n_steps)
    return pl.pallas_call(
        functools.partial(_ffn_kernel, tuple(job[2] for job in cast_jobs)),
        grid=(n_steps,),
        in_specs=[tok, _layer_spec((1, D_MODEL), layer), _layer_spec((1, D_MODEL), layer),
                  _resident_spec(w1.shape), _resident_spec(w2.shape)] + cast_in,
        out_specs=[tok] + cast_out,
        out_shape=[jax.ShapeDtypeStruct((t, D_MODEL), F32)] + cast_shapes,
        compiler_params=_compiler_params(1),
        name="ffn",
    )(x2, g_pre, g_post, w1, w2, *(job[0] for job in cast_jobs))


def kernel(x, norm_mix_pre, norm_mix_post, norm_ffn_pre, norm_ffn_post, w_in, attn_rel_bias, hgrn_lb_logits, hgrn_norm_g, gmlp_norm_g, gmlp_ws, gmlp_bs, lru_conv_w, lru_conv_b, lru_wa, lru_ba, lru_wx, lru_bx, lru_lambda, w_branch, w_out, w_ffn_in, w_ffn_out):
    b, s, d = x.shape
    x2 = x.reshape(b * s, d)
    w_mix = w_in[0, :, :N_MIX_COLS].astype(BF16)
    w_branch_rows = w_branch.reshape(DEPTH, N_BRANCH * MIX_W, D_MODEL)
    whole = lambda stack, layer: (stack, layer, ((0, stack.shape[2]),))
    row = lambda p: p.reshape(DEPTH, 1, p.shape[-1])
    g_mix_pre, g_mix_post = row(norm_mix_pre), row(norm_mix_post)
    g_ffn_pre, g_ffn_post = row(norm_ffn_pre), row(norm_ffn_post)
    bias_rows = _attn_bias_rows(attn_rel_bias)
    hg_consts = _hgrn_consts()
    hg_norm = row(hgrn_norm_g)
    gm_norm = row(gmlp_norm_g)
    gm_bias = jnp.repeat(jnp.swapaxes(gmlp_bs, 1, 2), HEAD_DIM, axis=2)
    lru_vec = jnp.concatenate(
        [lru_conv_w, lru_conv_b[:, None], lru_ba[:, None], lru_bx[:, None], lru_lambda[:, None]],
        axis=1)
    lru_w = _block_diag(jnp.stack([lru_wa, lru_wx], axis=1)).astype(BF16)
    mix_cols, gate_cols = (0, N_MIX_COLS), (N_MIX_COLS, w_in.shape[2])
    for l in range(DEPTH):
        mix_jobs = ((w_in, l, (gate_cols,)), whole(w_branch_rows, l), whole(w_out, l)) if l == 0 else ()
        z_att, z_rest, *staged = _in_proj(x2, l, g_mix_pre, w_mix, mix_jobs)
        if staged:
            w_gate, w_br, w_o = staged
        z3 = z_rest.reshape(b, s, N_MIX_COLS - N_ATT_COLS)
        o_a = _attention(z_att.reshape(b, s, N_ATT_COLS), l, bias_rows)
        o_b, o_c = _hgrn2_gmlp(z3, l, hgrn_lb_logits, hg_norm, *hg_consts, gm_norm, gmlp_ws, gm_bias)
        o_d = _rg_lru(z3, l, lru_vec, lru_w)
        outs = [o.reshape(b * s, MIX_W) for o in (o_a, o_b, o_c, o_d)]
        x2, w1, w2 = _mix_out(x2, outs, l, g_mix_pre, g_mix_post,
                              w_gate, w_br.reshape(N_BRANCH, MIX_W, D_MODEL), w_o,
                              (whole(w_ffn_in, l), whole(w_ffn_out, l)))
        next_jobs = ()
        if l + 1 < DEPTH:
            next_jobs = ((w_in, l + 1, (mix_cols, gate_cols)),
                         whole(w_branch_rows, l + 1), whole(w_out, l + 1))
        x2, *staged = _ffn(x2, l, g_ffn_pre, g_ffn_post, w1, w2, next_jobs)
        if staged:
            w_mix, w_gate, w_br, w_o = staged
    return x2.reshape(b, s, d)
```

```python
import functools

import numpy as np
import jax
import jax.numpy as jnp
from jax import lax
from jax.experimental import pallas as pl
from jax.experimental.pallas import tpu as pltpu

D_MODEL = 1024
DEPTH = 4
CHUNK = 64
EPS = 1e-6
N_HEADS = 4
HEAD_DIM = 64
MIX_W = N_HEADS * HEAD_DIM
N_BRANCH = 4
ATT_LEFT_CHUNKS = 8
REL_MAX = 256
NEG_BIG = -1e30
LOG_FLOOR = 1e-30
GM_BLOCK = 128
CONV_W = 4
LRU_C = 8.0
FFN_HIDDEN = 2816
N_MIX_COLS = 11 * MIX_W
N_ATT_COLS = 3 * MIX_W
LOG2_E = 1.4426950408889634
ATT_Q_SCALE = HEAD_DIM ** -0.5 * LOG2_E

F32 = jnp.float32
BF16 = jnp.bfloat16

VMEM_LIMIT_BYTES = 56 * 1024 * 1024
TOKEN_TILE = 1024
SUB_TILE = 256
ATT_QB = 256
ATT_PAD = ATT_LEFT_CHUNKS * CHUNK
ATT_KB = ATT_PAD + ATT_QB
ATT_ROLL_W = 1024
MXU_TILE = 256
FFN_HIDDEN_CHUNKS = ((0, 6 * MXU_TILE), (6 * MXU_TILE, FFN_HIDDEN))
FFN_PIECES_PER_CHUNK = 3
HG_CHUNK = 256
SUBLANES = 8
HG_BASE = 4
HG_LEVELS = (4, 8, 16, 32, 64, 128)


def _compiler_params(n_grid):
    return pltpu.CompilerParams(
        dimension_semantics=("arbitrary",) * n_grid,
        vmem_limit_bytes=VMEM_LIMIT_BYTES)


def _rms(x, g):
    return x * lax.rsqrt(jnp.mean(x * x, axis=-1, keepdims=True) + EPS) * g


def _gelu(x):
    c = 0.7978845608028654
    half_x = 0.5 * x
    return half_x + half_x * jnp.tanh(x * (c + (c * 0.044715) * (x * x)))


def _sigmoid(x):
    return 1.0 / (1.0 + jnp.exp(-x))


def _silu(x):
    return x * _sigmoid(x)


def _dot(a, b):
    return jnp.dot(a, b, preferred_element_type=F32)


def _dot_nt(a, b):
    return lax.dot_general(a, b, (((1,), (1,)), ((), ())), preferred_element_type=F32)


def _dot_tn(a, b):
    return lax.dot_general(a, b, (((0,), (0,)), ((), ())), preferred_element_type=F32)


def _split2(x):
    hi = x.astype(BF16)
    lo = (x - hi.astype(F32)).astype(BF16)
    return hi, lo


def _head_of_lane(shape, dim):
    return lax.broadcasted_iota(jnp.int32, shape, dim) // HEAD_DIM


def _resident_spec(shape):
    zeros = (0,) * len(shape)
    return pl.BlockSpec(tuple(shape), lambda i: zeros, pipeline_mode=pl.Buffered(1))


def _cast_specs(jobs, n_steps):
    in_specs, out_specs, out_shapes = [], [], []
    for stack, layer, col_ranges in jobs:
        _, n_rows, n_cols = stack.shape
        slab = n_rows // n_steps
        in_specs.append(pl.BlockSpec((None, slab, n_cols), lambda i, layer=layer: (layer, i, 0)))
        for lo, hi in col_ranges:
            out_specs.append(pl.BlockSpec((slab, hi - lo), lambda i: (i, 0)))
            out_shapes.append(jax.ShapeDtypeStruct((n_rows, hi - lo), BF16))
    return in_specs, out_specs, out_shapes


def _cast_slabs(col_ranges_per_job, src_refs, dst_refs):
    dst_refs = iter(dst_refs)
    for col_ranges, src_ref in zip(col_ranges_per_job, src_refs):
        for lo, hi in col_ranges:
            next(dst_refs)[...] = src_ref[:, lo:hi].astype(BF16)


def _layer_spec(shape, layer):
    zeros = (0,) * len(shape)
    return pl.BlockSpec((None,) + tuple(shape), lambda i: (layer,) + zeros,
                        pipeline_mode=pl.Buffered(1))


def _sub_tiles(n_rows):
    return tuple(pl.ds(r, SUB_TILE) for r in range(0, n_rows, SUB_TILE))


def _in_proj_kernel(cast_cols, x_ref, g_ref, w_ref, *refs):
    n_cast = len(cast_cols)
    za_ref, zr_ref = refs[n_cast:n_cast + 2]
    _cast_slabs(cast_cols, refs[:n_cast], refs[n_cast + 2:])
    for rows in _sub_tiles(x_ref.shape[0]):
        h = _rms(x_ref[rows, :], g_ref[...]).astype(BF16)
        z = _dot(h, w_ref[...])
        za_ref[rows, 0:MIX_W] = (z[:, 0:MIX_W] * ATT_Q_SCALE).astype(BF16)
        za_ref[rows, MIX_W:N_ATT_COLS] = z[:, MIX_W:N_ATT_COLS].astype(BF16)
        zr_ref[rows, :] = z[:, N_ATT_COLS:]


def _in_proj(x2, layer, g, w_mix, cast_jobs=()):
    t = x2.shape[0]
    n_rest = N_MIX_COLS - N_ATT_COLS
    n_steps = t // TOKEN_TILE
    cast_in, cast_out, cast_shapes = _cast_specs(cast_jobs, n_steps)
    return pl.pallas_call(
        functools.partial(_in_proj_kernel, tuple(job[2] for job in cast_jobs)),
        grid=(n_steps,),
        in_specs=[
            pl.BlockSpec((TOKEN_TILE, D_MODEL), lambda i: (i, 0)),
            _layer_spec((1, D_MODEL), layer),
            _resident_spec(w_mix.shape),
        ] + cast_in,
        out_specs=[pl.BlockSpec((TOKEN_TILE, N_ATT_COLS), lambda i: (i, 0)),
                   pl.BlockSpec((TOKEN_TILE, n_rest), lambda i: (i, 0))] + cast_out,
        out_shape=[jax.ShapeDtypeStruct((t, N_ATT_COLS), BF16),
                   jax.ShapeDtypeStruct((t, n_rest), F32)] + cast_shapes,
        compiler_params=_compiler_params(1),
        name="in_proj",
    )(x2, g, w_mix, *(job[0] for job in cast_jobs))


def _attn_kernel(q_ref, k_ref, v_ref, brow_ref, o_ref, bias_scr):
    s_len = q_ref.shape[0]

    @pl.when(pl.program_id(0) == 0)
    def _():
        qi = lax.broadcasted_iota(jnp.int32, (ATT_QB, ATT_KB), 0) // CHUNK
        kc = lax.broadcasted_iota(jnp.int32, (ATT_QB, ATT_KB), 1) // CHUNK
        band = (kc >= qi) & (kc <= qi + ATT_LEFT_CHUNKS)
        for h in range(N_HEADS):
            rows = jnp.broadcast_to(brow_ref[h:h + 1, :], (ATT_QB, ATT_ROLL_W))
            tile = pltpu.roll(rows, ATT_ROLL_W - ATT_QB, 1, stride=1, stride_axis=0)
            bias_scr[h] = jnp.where(band, tile[:, :ATT_KB], NEG_BIG)

    lane_head = _head_of_lane((ATT_QB, MIX_W), 1)
    head_rows = [jnp.where(_head_of_lane((1, MIX_W), 1) == h, 1.0, 0.0).astype(BF16)
                 for h in range(N_HEADS)]

    def block(r0, k0, n_keys):
        q = q_ref[pl.ds(r0, ATT_QB), :]
        kb = k_ref[pl.ds(k0, n_keys), :]
        vb = v_ref[pl.ds(k0, n_keys), :]
        acc = jnp.zeros((ATT_QB, MIX_W), F32)
        for h in range(N_HEADS):
            s = _dot_nt(q * head_rows[h], kb) + bias_scr[h, :, pl.ds(ATT_KB - n_keys, n_keys)]
            m = jnp.max(s, axis=-1, keepdims=True)
            p = jnp.exp2(s - m)
            l = jnp.sum(p, axis=-1, keepdims=True)
            o = _dot(p.astype(BF16), vb)
            acc = acc + jnp.where(lane_head == h, o * (1.0 / l), 0.0)
        o_ref[pl.ds(r0, ATT_QB), :] = acc.astype(o_ref.dtype)

    n_head_blocks = ATT_PAD // ATT_QB
    for j in range(n_head_blocks):
        block(j * ATT_QB, 0, (j + 1) * ATT_QB)

    def body(j, carry):
        r0 = pl.multiple_of(j * ATT_QB, ATT_QB)
        block(r0, r0 - ATT_PAD, ATT_KB)
        return carry

    lax.fori_loop(n_head_blocks, s_len // ATT_QB, body, 0, unroll=6)


def _attn_bias_rows(rel_bias):
    dist = ATT_PAD + ATT_QB - np.arange(ATT_ROLL_W)
    idx = np.clip(dist, -(CHUNK - 1), REL_MAX) + (CHUNK - 1)
    return rel_bias.astype(F32)[:, :, idx] * LOG2_E


def _attention(za3, layer, bias_rows):
    b, s, _ = za3.shape
    col_spec = lambda c: pl.BlockSpec((None, s, MIX_W), lambda i, c=c: (i, 0, c))
    return pl.pallas_call(
        _attn_kernel,
        grid=(b,),
        in_specs=[col_spec(0), col_spec(1), col_spec(2),
                  _layer_spec((N_HEADS, ATT_ROLL_W), layer)],
        out_specs=pl.BlockSpec((None, s, MIX_W), lambda i: (i, 0, 0)),
        out_shape=jax.ShapeDtypeStruct((b, s, MIX_W), BF16),
        scratch_shapes=[pltpu.VMEM((N_HEADS, ATT_QB, ATT_KB), F32)],
        compiler_params=_compiler_params(1),
        name="attention",
    )(za3, za3, za3, bias_rows)


def _hgrn_half_rows(g, right):
    off = g if right else 0
    return np.concatenate([np.arange(2 * g * b + off, 2 * g * b + off + g)
                           for b in range(HG_CHUNK // (2 * g))])


def _hgrn_coef_matrix():
    r = np.arange(HG_CHUNK)[None, :]
    mats = [r <= np.arange(HG_CHUNK)[:, None]]
    for g in HG_LEVELS:
        for right in (True, False):
            t = _hgrn_half_rows(g, right)[:, None]
            bnd = (t // (2 * g)) * (2 * g) + g - 1
            mats.append((r > bnd) & (r <= t) if right else (r > t) & (r <= bnd))
    return np.concatenate(mats, axis=0).astype(np.float32)


def _hgrn_level_masks():
    half = HG_CHUNK // 2
    tc = np.arange(half)[:, None]
    sc = np.arange(2 * half)[None, :] % half
    return np.stack([(tc // g) == (sc // g) for g in HG_LEVELS]).astype(np.float32)


def _hgrn_consts():
    return (jnp.asarray(_hgrn_coef_matrix(), dtype=BF16), jnp.asarray(_hgrn_level_masks()))


def _hgrn_gmlp_kernel(layer, q_ref, f_ref, i_ref, g_ref, gu_ref, gv_ref, lbl_ref, ng_ref,
                      coef_ref, lmask_ref, gm_ng_ref, gm_ws_ref, gm_bias_ref, o_ref, gm_o_ref):
    s_len = q_ref.shape[0]
    gm_w_cat = _gmlp_weights(gm_ws_ref)
    c = HG_CHUNK
    half = c // 2
    logits = lbl_ref[...]
    e = jnp.exp(logits - jnp.max(logits, axis=0, keepdims=True))
    p = e / jnp.sum(e, axis=0, keepdims=True)
    lb = jnp.zeros((1, MIX_W), F32)
    for l in range(1, layer + 1):
        lb = lb + p[l:l + 1, :]

    same_head = (_head_of_lane((MIX_W, MIX_W), 0) == _head_of_lane((MIX_W, MIX_W), 1))
    head_ones = jnp.where(same_head, 1.0, 0.0).astype(BF16)
    head_rows = [jnp.where(_head_of_lane((1, MIX_W), 1) == h, 1.0, 0.0).astype(BF16)
                 for h in range(N_HEADS)]
    group_shape = (c // SUBLANES, SUBLANES, MIX_W)
    row_in_group = lax.broadcasted_iota(jnp.int32, group_shape, 1) % HG_BASE

    def body(j, st):
        r0 = pl.multiple_of(j * c, c)
        fz = f_ref[pl.ds(r0, c), :]
        qf = _silu(q_ref[pl.ds(r0, c), :])
        sg = _sigmoid(fz)
        f = lb + (1.0 - lb) * sg
        lf = jnp.log2(jnp.maximum(f, LOG_FLOOR))
        kf = (1.0 - lb) * (1.0 - sg)
        v = i_ref[pl.ds(r0, c), :]
        vb = v.astype(BF16)

        lf_hi, lf_lo = _split2(lf)
        pc = _dot(coef_ref[0:c, :], lf_hi) + _dot(coef_ref[0:c, :], lf_lo)
        sums = _dot(coef_ref[c:, :], lf_hi)
        ptot = pc[c - 1:c, :]

        qd = (qf * jnp.exp2(pc)).astype(BF16)
        terms = [_dot_nt(qd, st.astype(BF16))]
        kd = (kf * jnp.exp2(ptot - pc)).astype(BF16)
        st_next = st * jnp.exp2(ptot) + jnp.where(same_head, _dot_tn(vb, kd), 0.0)

        p3 = pc.reshape(group_shape)
        q3, k3, v3 = (x.reshape(group_shape) for x in (qf, kf, v))

        def group_term(d):
            if d == 0:
                prod, vd = q3 * k3, v3
            else:
                e = jnp.exp2(jnp.minimum(p3 - pltpu.roll(p3, d, 1), 0.0))
                prod = jnp.where(row_in_group >= d, q3 * pltpu.roll(k3, d, 1) * e, 0.0)
                vd = pltpu.roll(v3, d, 1)
            w = _dot(prod.reshape(c, MIX_W).astype(BF16), head_ones)
            return w * vd.reshape(c, MIX_W)

        def level_term(li, g):
            n_blocks = half // g

            def half_rows(x, off):
                return jnp.concatenate(
                    [x[2 * g * b + off:2 * g * b + off + g] for b in range(n_blocks)], axis=0)

            cq = sums[li * c:li * c + half]
            ck = sums[li * c + half:(li + 1) * c]
            qt = (half_rows(qf, g) * jnp.exp2(cq)).astype(BF16)
            kt = (half_rows(kf, 0) * jnp.exp2(ck)).astype(BF16)
            vt = half_rows(v, 0).astype(BF16)
            out = jnp.zeros((half, MIX_W), F32)
            for h in range(0, N_HEADS, 2):
                kw = jnp.concatenate([kt * head_rows[h], kt * head_rows[h + 1]], axis=0)
                vw = jnp.concatenate([vt * head_rows[h], vt * head_rows[h + 1]], axis=0)
                sc = _dot_nt(qt, kw) * lmask_ref[li]
                out = out + _dot(sc.astype(BF16), vw)
            zero = jnp.zeros((g, MIX_W), F32)
            return jnp.concatenate(
                [piece for b in range(n_blocks) for piece in (zero, out[b * g:(b + 1) * g])], axis=0)

        gm_blocks = list(range(0, c, GM_BLOCK))
        for i in range(max(HG_BASE, len(HG_LEVELS))):
            if i < HG_BASE:
                terms.append(group_term(i))
            if i < len(HG_LEVELS):
                terms.append(level_term(i, HG_LEVELS[i]))
            if i % 2 == 1 and gm_blocks:
                _gmlp_block(r0 + gm_blocks.pop(0), gu_ref, gv_ref, gm_ng_ref, gm_bias_ref,
                            gm_o_ref, gm_w_cat, head_rows)
        while len(terms) > 1:
            terms = [a + b for a, b in zip(terms[0::2], terms[1::2])] + terms[len(terms) & ~1:]
        o = terms[0]

        ms = _dot((o * o).astype(BF16), head_ones) * (1.0 / HEAD_DIM)
        o = o * lax.rsqrt(ms + EPS) * ng_ref[...]
        o = o * _silu(g_ref[pl.ds(r0, c), :])
        o_ref[pl.ds(r0, c), :] = o.astype(o_ref.dtype)
        return st_next

    lax.fori_loop(0, s_len // c, body, jnp.zeros((MIX_W, MIX_W), F32), unroll=8)


def _hgrn2_gmlp(z3, layer, lb_logits, hg_norm, coef, level_masks, gm_norm, gm_ws, gm_bias):
    b, s, _ = z3.shape
    col_spec = lambda c: pl.BlockSpec((None, s, MIX_W), lambda i, c=c: (i, 0, c))
    out_spec = pl.BlockSpec((None, s, MIX_W), lambda i: (i, 0, 0))
    out_shape = jax.ShapeDtypeStruct((b, s, MIX_W), BF16)
    return pl.pallas_call(
        functools.partial(_hgrn_gmlp_kernel, layer),
        grid=(b,),
        in_specs=[col_spec(0), col_spec(1), col_spec(2), col_spec(3), col_spec(4), col_spec(5),
                  pl.BlockSpec((DEPTH, MIX_W), lambda i: (0, 0)),
                  _layer_spec((1, MIX_W), layer),
                  pl.BlockSpec(coef.shape, lambda i: (0, 0)),
                  pl.BlockSpec(level_masks.shape, lambda i: (0, 0, 0)),
                  _layer_spec((1, MIX_W), layer),
                  _layer_spec((N_HEADS, GM_BLOCK, GM_BLOCK), layer),
                  _layer_spec((GM_BLOCK, MIX_W), layer)],
        out_specs=[out_spec, out_spec],
        out_shape=[out_shape, out_shape],
        compiler_params=_compiler_params(1),
        name="hgrn2_gmlp",
    )(z3, z3, z3, z3, z3, z3, lb_logits, hg_norm, coef, level_masks, gm_norm, gm_ws, gm_bias)


def _gmlp_weights(ws_ref):
    prow = lax.broadcasted_iota(jnp.int32, (GM_BLOCK, GM_BLOCK), 0)
    pcol = lax.broadcasted_iota(jnp.int32, (GM_BLOCK, GM_BLOCK), 1)
    return jnp.concatenate(
        [jnp.where(pcol <= prow, ws_ref[g], 0.0).astype(BF16) for g in range(N_HEADS)], axis=1)


def _gmlp_block(r0, u_ref, v_ref, ng_ref, bias_ref, o_ref, w_cat, group_rows):
    rows = pl.ds(r0, GM_BLOCK)
    vn = _rms(_gelu(v_ref[rows, :]), ng_ref[...]).astype(BF16)
    vn_cat = jnp.concatenate([vn * group_rows[g] for g in range(N_HEADS)], axis=0)
    mixed = bias_ref[...] + _dot(w_cat, vn_cat)
    o_ref[rows, :] = (_gelu(u_ref[rows, :]) * mixed).astype(o_ref.dtype)


LRU_ROW_CONV_B, LRU_ROW_BA, LRU_ROW_BX, LRU_ROW_LAM = 4, 5, 6, 7
LRU_VEC_ROWS = 8


def _lru_kernel(x_ref, gate_ref, vec_ref, w_ref, o_ref, a_ref, b_ref):
    s_len = x_ref.shape[0]
    vrow = lambda k: vec_ref[k:k + 1, :]
    x = x_ref[...]
    t_head = lax.broadcasted_iota(jnp.int32, (SUBLANES, MIX_W), 0)

    def first_tile(fn, v):
        return jnp.concatenate([fn(v[0:SUBLANES]), v[SUBLANES:]], axis=0)

    xc = vrow(LRU_ROW_CONV_B) + x * vrow(CONV_W - 1)
    for d in range(1, CONV_W):
        xs = first_tile(lambda v, d=d: jnp.where(t_head >= d, v, 0.0), pltpu.roll(x, d, 0))
        xc = xc + xs * vrow(CONV_W - 1 - d)
    xcb = xc.astype(BF16)
    r = _sigmoid(_dot(xcb, w_ref[0]) + vrow(LRU_ROW_BA))
    ig = _sigmoid(_dot(xcb, w_ref[1]) + vrow(LRU_ROW_BX))
    nl = -vrow(LRU_ROW_LAM)
    softplus = jnp.maximum(nl, 0.0) + jnp.log(1.0 + jnp.exp(-jnp.abs(nl)))
    a_full = jnp.exp(r * (-LRU_C * softplus))
    mult = jnp.sqrt(jnp.maximum(1.0 - a_full * a_full, 0.0))
    mult = first_tile(lambda v: jnp.where(t_head == 0, 1.0, v), mult)
    tiles = (s_len // SUBLANES, SUBLANES, MIX_W)
    a = a_full.reshape(tiles)
    b = (mult * (ig * xc)).reshape(tiles)
    row_in_tile = lax.broadcasted_iota(jnp.int32, tiles, 1)
    d = 1
    while d < SUBLANES:
        ok = row_in_tile >= d
        a_prev = jnp.where(ok, pltpu.roll(a, d, 1), 1.0)
        b_prev = jnp.where(ok, pltpu.roll(b, d, 1), 0.0)
        b = a * b_prev + b
        a = a * a_prev
        d *= 2
    a_ref[...] = a.reshape(s_len, MIX_W)
    b_ref[...] = b.reshape(s_len, MIX_W)
    carry = jnp.zeros((1, MIX_W), F32)
    for r0 in range(0, s_len, SUBLANES):
        rows = pl.ds(r0, SUBLANES)
        h = b_ref[rows, :] + a_ref[rows, :] * carry
        b_ref[rows, :] = h
        carry = h[SUBLANES - 1:SUBLANES, :]
    o_ref[...] = (b_ref[...] * _gelu(gate_ref[...])).astype(o_ref.dtype)


def _block_diag(w):
    eye = jnp.eye(N_HEADS, dtype=w.dtype)
    bd = jnp.einsum('...hij,hg->...higj', w, eye)
    return bd.reshape(w.shape[:-3] + (MIX_W, MIX_W))


def _rg_lru(z3, layer, vec, w_bd):
    b, s, _ = z3.shape
    col_spec = lambda c: pl.BlockSpec((None, s, MIX_W), lambda i, c=c: (i, 0, c))
    return pl.pallas_call(
        _lru_kernel,
        grid=(b,),
        in_specs=[col_spec(6), col_spec(7),
                  _layer_spec((LRU_VEC_ROWS, MIX_W), layer),
                  _layer_spec((2, MIX_W, MIX_W), layer)],
        out_specs=pl.BlockSpec((None, s, MIX_W), lambda i: (i, 0, 0)),
        out_shape=jax.ShapeDtypeStruct((b, s, MIX_W), BF16),
        scratch_shapes=[pltpu.VMEM((s, MIX_W), F32), pltpu.VMEM((s, MIX_W), F32)],
        compiler_params=_compiler_params(1),
        name="rg_lru",
    )(z3, z3, vec, w_bd)


def _mix_out_kernel(cast_cols, x_ref, oa_ref, ob_ref, oc_ref, od_ref, gpre_ref, gpost_ref,
                    wg_ref, wb_ref, wo_ref, *refs):
    n_cast = len(cast_cols)
    out_ref = refs[n_cast]
    _cast_slabs(cast_cols, refs[:n_cast], refs[n_cast + 1:])
    for rows in _sub_tiles(x_ref.shape[0]):
        x = x_ref[rows, :]
        h = _rms(x, gpre_ref[...]).astype(BF16)
        merged = jnp.zeros(x.shape, F32)
        for n, o_ref in enumerate((oa_ref, ob_ref, oc_ref, od_ref)):
            gate = _dot(h, wg_ref[:, n * D_MODEL:(n + 1) * D_MODEL])
            proj = _dot(o_ref[rows, :], wb_ref[n])
            merged = merged + _sigmoid(gate) * proj
        y = _dot(merged.astype(BF16), wo_ref[...])
        out_ref[rows, :] = x + _rms(y, gpost_ref[...])


def _mix_out(x2, outs, layer, g_pre, g_post, w_gate, w_branch, w_out, cast_jobs=()):
    t = x2.shape[0]
    n_steps = t // TOKEN_TILE
    tok = lambda w: pl.BlockSpec((TOKEN_TILE, w), lambda i: (i, 0))
    cast_in, cast_out, cast_shapes = _cast_specs(cast_jobs, n_steps)
    return pl.pallas_call(
        functools.partial(_mix_out_kernel, tuple(job[2] for job in cast_jobs)),
        grid=(n_steps,),
        in_specs=[tok(D_MODEL), tok(MIX_W), tok(MIX_W), tok(MIX_W), tok(MIX_W),
                  _layer_spec((1, D_MODEL), layer), _layer_spec((1, D_MODEL), layer),
                  _resident_spec(w_gate.shape), _resident_spec(w_branch.shape),
                  _resident_spec(w_out.shape)] + cast_in,
        out_specs=[tok(D_MODEL)] + cast_out,
        out_shape=[jax.ShapeDtypeStruct((t, D_MODEL), F32)] + cast_shapes,
        compiler_params=_compiler_params(1),
        name="mix_out",
    )(x2, *outs, g_pre, g_post, w_gate, w_branch, w_out, *(job[0] for job in cast_jobs))


def _ffn_kernel(cast_cols, x_ref, gpre_ref, gpost_ref, w1_hbm, w2_hbm, *refs):
    w1_ref, w2_ref, sems = refs[-3:]
    refs = refs[:-3]
    n_cast = len(cast_cols)
    out_ref = refs[n_cast]
    first_step = pl.program_id(0) == 0

    def piece_copy(chunk, part):
        lo, hi = FFN_HIDDEN_CHUNKS[chunk]
        if part == 2:
            src, dst = w2_hbm.at[pl.ds(lo, hi - lo), :], w2_ref.at[pl.ds(lo, hi - lo), :]
        else:
            cols = pl.ds(part * FFN_HIDDEN + lo, hi - lo)
            src, dst = w1_hbm.at[:, cols], w1_ref.at[:, cols]
        return pltpu.make_async_copy(src, dst, sems.at[chunk * FFN_PIECES_PER_CHUNK + part])

    def wait_piece(chunk, part):
        @pl.when(first_step)
        def _():
            piece_copy(chunk, part).wait()

    @pl.when(first_step)
    def _():
        for chunk in range(len(FFN_HIDDEN_CHUNKS)):
            for part in range(FFN_PIECES_PER_CHUNK):
                piece_copy(chunk, part).start()

    _cast_slabs(cast_cols, refs[:n_cast], refs[n_cast + 1:])
    for slab, rows in enumerate(_sub_tiles(x_ref.shape[0])):
        x = x_ref[rows, :]
        h = _rms(x, gpre_ref[...]).astype(BF16)
        y = jnp.zeros(x.shape, F32)
        for chunk, (lo, hi) in enumerate(FFN_HIDDEN_CHUNKS):
            if slab == 0:
                wait_piece(chunk, 0)
            gt = _dot(h, w1_ref[:, lo:hi])
            if slab == 0:
                wait_piece(chunk, 1)
            up = _dot(h, w1_ref[:, FFN_HIDDEN + lo:FFN_HIDDEN + hi])
            act = (_silu(gt) * up).astype(BF16)
            if slab == 0:
                wait_piece(chunk, 2)
            y = y + _dot(act, w2_ref[lo:hi, :])
        out_ref[rows, :] = x + _rms(y, gpost_ref[...])


def _ffn(x2, layer, g_pre, g_post, w1, w2, cast_jobs=()):
    t = x2.shape[0]
    n_steps = t // TOKEN_TILE
    tok = pl.BlockSpec((TOKEN_TILE, D_MODEL), lambda i: (i, 0))
    cast_in, cast_out, cast_shapes = _cast_specs(cast_jobs, n_steps)
    return pl.pallas_call(
        functools.partial(_ffn_kernel, tuple(job[2] for job in cast_jobs)),
        grid=(n_steps,),
        in_specs=[tok, _layer_spec((1, D_MODEL), layer), _layer_spec((1, D_MODEL), layer),
                  pl.BlockSpec(memory_space=pl.ANY), pl.BlockSpec(memory_space=pl.ANY)] + cast_in,
        out_specs=[tok] + cast_out,
        out_shape=[jax.ShapeDtypeStruct((t, D_MODEL), F32)] + cast_shapes,
        scratch_shapes=[pltpu.VMEM(w1.shape, BF16), pltpu.VMEM(w2.shape, BF16),
                        pltpu.SemaphoreType.DMA((len(FFN_HIDDEN_CHUNKS) * FFN_PIECES_PER_CHUNK,))],
        compiler_params=_compiler_params(1),
        name="ffn",
    )(x2, g_pre, g_post, w1, w2, *(job[0] for job in cast_jobs))


def kernel(x, norm_mix_pre, norm_mix_post, norm_ffn_pre, norm_ffn_post, w_in, attn_rel_bias, hgrn_lb_logits, hgrn_norm_g, gmlp_norm_g, gmlp_ws, gmlp_bs, lru_conv_w, lru_conv_b, lru_wa, lru_ba, lru_wx, lru_bx, lru_lambda, w_branch, w_out, w_ffn_in, w_ffn_out):
    b, s, d = x.shape
    x2 = x.reshape(b * s, d)
    w_mix = w_in[0, :, :N_MIX_COLS].astype(BF16)
    w_branch_rows = w_branch.reshape(DEPTH, N_BRANCH * MIX_W, D_MODEL)
    whole = lambda stack, layer: (stack, layer, ((0, stack.shape[2]),))
    row = lambda p: p.reshape(DEPTH, 1, p.shape[-1])
    g_mix_pre, g_mix_post = row(norm_mix_pre), row(norm_mix_post)
    g_ffn_pre, g_ffn_post = row(norm_ffn_pre), row(norm_ffn_post)
    bias_rows = _attn_bias_rows(attn_rel_bias)
    hg_consts = _hgrn_consts()
    hg_norm = row(hgrn_norm_g)
    gm_norm = row(gmlp_norm_g)
    gm_bias = jnp.repeat(jnp.swapaxes(gmlp_bs, 1, 2), HEAD_DIM, axis=2)
    lru_vec = jnp.concatenate(
        [lru_conv_w, lru_conv_b[:, None], lru_ba[:, None], lru_bx[:, None], lru_lambda[:, None]],
        axis=1)
    lru_w = _block_diag(jnp.stack([lru_wa, lru_wx], axis=1)).astype(BF16)
    mix_cols, gate_cols = (0, N_MIX_COLS), (N_MIX_COLS, w_in.shape[2])
    for l in range(DEPTH):
        mix_jobs = ((w_in, l, (gate_cols,)), whole(w_branch_rows, l), whole(w_out, l)) if l == 0 else ()
        z_att, z_rest, *staged = _in_proj(x2, l, g_mix_pre, w_mix, mix_jobs)
        if staged:
            w_gate, w_br, w_o = staged
        z3 = z_rest.reshape(b, s, N_MIX_COLS - N_ATT_COLS)
        o_a = _attention(z_att.reshape(b, s, N_ATT_COLS), l, bias_rows)
        o_b, o_c = _hgrn2_gmlp(z3, l, hgrn_lb_logits, hg_norm, *hg_consts, gm_norm, gmlp_ws, gm_bias)
        o_d = _rg_lru(z3, l, lru_vec, lru_w)
        outs = [o.reshape(b * s, MIX_W) for o in (o_a, o_b, o_c, o_d)]
        x2, w1, w2 = _mix_out(x2, outs, l, g_mix_pre, g_mix_post,
                              w_gate, w_br.reshape(N_BRANCH, MIX_W, D_MODEL), w_o,
                              (whole(w_ffn_in, l), whole(w_ffn_out, l)))
        next_jobs = ()
        if l + 1 < DEPTH:
            next_jobs = ((w_in, l + 1, (mix_cols, gate_cols)),
                         whole(w_branch_rows, l + 1), whole(w_out, l + 1))
        x2, *staged = _ffn(x2, l, g_ffn_pre, g_ffn_post, w1, w2, next_jobs)
        if staged:
            w_mix, w_gate, w_br, w_o = staged
    return x2.reshape(b, s, d)
```
